```python
import math
import jax, jax.numpy as jnp
from jax import lax
import numpy as np

D_MODEL = 2048
BATCH = 8
SEQ = 2048
DEPTH = 2
DEC_BATCH = 32
DEC_SEQ = 4
PAST_LEN = 8192
PAGE_SIZE = 128

MIX_WIDTH = D_MODEL
ATTN_WIDTH = D_MODEL // 2
CONV_WIDTH = D_MODEL // 4
LRU_WIDTH = MIX_WIDTH - ATTN_WIDTH - CONV_WIDTH
N_HEADS = 8
V_HEAD_DIM = ATTN_WIDTH // N_HEADS
QK_HEAD_DIM = V_HEAD_DIM // 2
ROPE_DIM = QK_HEAD_DIM // 4
ROPE_THETA = 500000.0
ATTN_SCALE = 1.0 / math.sqrt(QK_HEAD_DIM)
BLOCK_Q = 128
CONV_KERNEL = 31
LRU_HEADS = 8
LRU_HEAD_DIM = LRU_WIDTH // LRU_HEADS
LRU_CONV = 4
LRU_C = 8.0
D_FF = -(-8 * D_MODEL // (3 * 256)) * 256
EPS = 1e-6
NEG_INF = -1e30
Q_COLS = N_HEADS * 2 * QK_HEAD_DIM
K_COLS = N_HEADS * 2 * QK_HEAD_DIM
V_COLS = N_HEADS * V_HEAD_DIM
CONV_COLS = 2 * CONV_WIDTH
LRU_COLS = 2 * LRU_WIDTH
IN_COLS = Q_COLS + K_COLS + V_COLS + CONV_COLS + LRU_COLS
SPLITS = [Q_COLS, Q_COLS + K_COLS, Q_COLS + K_COLS + V_COLS, Q_COLS + K_COLS + V_COLS + CONV_COLS]

kernel_name = "hymba_diffattn_conformer_rglru_step"


def rmsnorm(x, g):
    xf = x.astype(jnp.float32)
    y = xf * lax.rsqrt(jnp.mean(xf * xf, axis=-1, keepdims=True) + EPS)
    return (y * g.astype(jnp.float32)).astype(x.dtype)


def layernorm(x, g, b):
    xf = x.astype(jnp.float32)
    mu = jnp.mean(xf, axis=-1, keepdims=True)
    var = jnp.mean(jnp.square(xf - mu), axis=-1, keepdims=True)
    y = (xf - mu) * lax.rsqrt(var + EPS)
    return (y * g.astype(jnp.float32) + b.astype(jnp.float32)).astype(x.dtype)


def partial_rope(x, pos):
    half = ROPE_DIM // 2
    inv_freq = ROPE_THETA ** (-jnp.arange(0, ROPE_DIM, 2, dtype=jnp.float32) / ROPE_DIM)
    ang = pos.astype(jnp.float32)[:, None] * inv_freq[None, :]
    cos = jnp.cos(ang)[:, None, None, :]
    sin = jnp.sin(ang)[:, None, None, :]
    xr = x[..., :ROPE_DIM].astype(jnp.float32)
    x1, x2 = xr[..., :half], xr[..., half:]
    rot = jnp.concatenate([x1 * cos - x2 * sin, x2 * cos + x1 * sin], axis=-1).astype(x.dtype)
    return jnp.concatenate([rot, x[..., ROPE_DIM:]], axis=-1)


def causal_dwconv(u, buf, w, b):
    width = w.shape[0]
    padded = jnp.concatenate([buf.astype(u.dtype), u], axis=1)
    out = lax.conv_general_dilated(
        padded, w[:, None, :].astype(u.dtype), window_strides=(1,), padding='VALID',
        dimension_numbers=('NWC', 'WIO', 'NWC'), feature_group_count=u.shape[-1])
    return out + b.astype(u.dtype), padded[:, -(width - 1):]


def diff_combine(s, lam):
    p = jax.nn.softmax(s, axis=-1)
    return p[:, :, 0] - lam * p[:, :, 1]


def diff_attn_prompt(q, k, v, lam):
    B, T = q.shape[0], q.shape[1]
    nb = T // BLOCK_Q
    qb = q.reshape(B, nb, BLOCK_Q, N_HEADS, 2, QK_HEAD_DIM).swapaxes(0, 1)
    kpos = jnp.arange(T)

    def block(args):
        qi, idx = args
        s = jnp.einsum('bqhcd,bkhcd->bhcqk', qi, k).astype(jnp.float32) * ATTN_SCALE
        qpos = idx * BLOCK_Q + jnp.arange(BLOCK_Q)
        s = jnp.where(kpos[None, :] <= qpos[:, None], s, NEG_INF)
        a = diff_combine(s, lam).astype(v.dtype)
        return jnp.einsum('bhqk,bkhd->bqhd', a, v)

    o = lax.map(block, (qb, jnp.arange(nb)))
    return o.swapaxes(0, 1).reshape(B, T, N_HEADS, V_HEAD_DIM)


def diff_attn_sample(q, k_new, v_new, k_past, v_past, lam):
    T = q.shape[1]
    P = k_past.shape[1]
    s_past = jnp.einsum('bqhcd,bkhcd->bhcqk', q, k_past).astype(jnp.float32) * ATTN_SCALE
    s_new = jnp.einsum('bqhcd,bkhcd->bhcqk', q, k_new).astype(jnp.float32) * ATTN_SCALE
    s_new = jnp.where(jnp.tril(jnp.ones((T, T), dtype=bool)), s_new, NEG_INF)
    a = diff_combine(jnp.concatenate([s_past, s_new], axis=-1), lam).astype(v_new.dtype)
    return (jnp.einsum('bhqk,bkhd->bqhd', a[..., :P], v_past)
            + jnp.einsum('bhqk,bkhd->bqhd', a[..., P:], v_new))


def conv_module(u2, buf, w, b, ln_g, ln_b):
    a, g = jnp.split(u2, 2, axis=-1)
    u = a * jax.nn.sigmoid(g)
    c, new_buf = causal_dwconv(u, buf, w, b)
    return jax.nn.silu(layernorm(c, ln_g, ln_b)), new_buf


def rg_lru_block(u2, conv_buf, h0, conv_w, conv_b, wa, ba, wx, bx, lam):
    gate_in, xb = jnp.split(u2, 2, axis=-1)
    gate = jax.nn.gelu(gate_in)
    xc, new_conv_buf = causal_dwconv(xb, conv_buf, conv_w, conv_b)
    B, T, _ = xc.shape
    xh = xc.reshape(B, T, LRU_HEADS, LRU_HEAD_DIM)
    r = jax.nn.sigmoid(jnp.einsum('bthi,hij->bthj', xh, wa).reshape(B, T, LRU_WIDTH) + ba)
    i = jax.nn.sigmoid(jnp.einsum('bthi,hij->bthj', xh, wx).reshape(B, T, LRU_WIDTH) + bx)
    log_a = -LRU_C * r.astype(jnp.float32) * jax.nn.softplus(-lam.astype(jnp.float32))
    a = jnp.exp(log_a)
    bterm = jnp.sqrt(-jnp.expm1(2.0 * log_a)) * (i * xc).astype(jnp.float32)

    def step(h, ab):
        at, bt = ab
        h = at * h + bt
        return h, h

    h_last, hs = lax.scan(step, h0.astype(jnp.float32), (a.swapaxes(0, 1), bterm.swapaxes(0, 1)))
    y = hs.swapaxes(0, 1).astype(xc.dtype) * gate
    return y, new_conv_buf, h_last


def trunk(x, start_pos, p, past):
    B, T, _ = x.shape
    pos = start_pos + jnp.arange(T, dtype=jnp.int32)
    ks, vs, cbs, lcbs, hs = [], [], [], [], []
    for l in range(DEPTH):
        lam_init = 0.8 - 0.6 * math.exp(-0.3 * l)
        xn = rmsnorm(x, p['norm_mix'][l])
        proj = jnp.einsum('btd,dc->btc', xn, p['w_in'][l])
        q, k, v, uc, ul = jnp.split(proj, SPLITS, axis=-1)
        q = partial_rope(rmsnorm(q.reshape(B, T, N_HEADS, 2, QK_HEAD_DIM), p['q_norm'][l]), pos)
        k = partial_rope(rmsnorm(k.reshape(B, T, N_HEADS, 2, QK_HEAD_DIM), p['k_norm'][l]), pos)
        v = v.reshape(B, T, N_HEADS, V_HEAD_DIM)
        lam = (jnp.exp(jnp.sum(p['lambda_q1'][l].astype(jnp.float32) * p['lambda_k1'][l].astype(jnp.float32)))
               - jnp.exp(jnp.sum(p['lambda_q2'][l].astype(jnp.float32) * p['lambda_k2'][l].astype(jnp.float32)))
               + lam_init)
        if past is None:
            o = diff_attn_prompt(q, k, v, lam)
            conv_buf = jnp.zeros((B, CONV_KERNEL - 1, CONV_WIDTH), x.dtype)
            lru_buf = jnp.zeros((B, LRU_CONV - 1, LRU_WIDTH), x.dtype)
            h0 = jnp.zeros((B, LRU_WIDTH), jnp.float32)
        else:
            cache_k, cache_v, state_conv, state_lru_conv, state_lru_h, page_table = past
            k_past = cache_k[l, page_table].reshape(B, -1, N_HEADS, 2, QK_HEAD_DIM)
            v_past = cache_v[l, page_table].reshape(B, -1, N_HEADS, V_HEAD_DIM)
            o = diff_attn_sample(q, k, v, k_past.astype(q.dtype), v_past.astype(v.dtype), lam)
            conv_buf = state_conv[l]
            lru_buf = state_lru_conv[l]
            h0 = state_lru_h[l]
        o_attn = (rmsnorm(o, p['subln'][l]) * (1.0 - lam_init)).reshape(B, T, ATTN_WIDTH)
        o_conv, new_cb = conv_module(uc, conv_buf, p['conv_w'][l], p['conv_b'][l],
                                     p['conv_ln_g'][l], p['conv_ln_b'][l])
        o_lru, new_lcb, h_last = rg_lru_block(ul, lru_buf, h0, p['lru_conv_w'][l], p['lru_conv_b'][l],
                                              p['lru_wa'][l], p['lru_ba'][l], p['lru_wx'][l],
                                              p['lru_bx'][l], p['lru_lambda'][l])
        mixed = jnp.concatenate([o_attn, o_conv, o_lru], axis=-1)
        x = x + jnp.einsum('btc,cd->btd', mixed, p['w_out'][l])
        xn = rmsnorm(x, p['norm_ffn'][l])
        hid = jax.nn.silu(jnp.einsum('btd,df->btf', xn, p['w_ffn_gate'][l])) * \
            jnp.einsum('btd,df->btf', xn, p['w_ffn_up'][l])
        x = x + jnp.einsum('btf,fd->btd', hid, p['w_ffn_down'][l])
        ks.append(k.reshape(B, T, N_HEADS, 2 * QK_HEAD_DIM))
        vs.append(v)
        cbs.append(new_cb)
        lcbs.append(new_lcb)
        hs.append(h_last)
    return x, jnp.stack(ks), jnp.stack(vs), jnp.stack(cbs), jnp.stack(lcbs), jnp.stack(hs)


def setup_inputs(seed: int = 0) -> dict:
    key = jax.random.key(seed)
    ks = jax.random.split(key, 40)
    f32 = jnp.float32
    n_pages = PAST_LEN // PAGE_SIZE
    n_phys = (DEC_BATCH * n_pages * 5) // 4

    def nrm(k, shape, scale):
        return jax.random.normal(k, shape, f32) * scale

    def gain(k, shape):
        return 1.0 + nrm(k, shape, 0.02)

    a0 = jax.random.uniform(ks[27], (DEPTH, LRU_WIDTH), f32, 0.9, 0.999)
    a_base = a0 ** (1.0 / LRU_C)
    return {
        "x_prompt": nrm(ks[0], (BATCH, SEQ, D_MODEL), 1.0),
        "x_sample": nrm(ks[1], (DEC_BATCH, DEC_SEQ, D_MODEL), 1.0),
        "cache_k": nrm(ks[2], (DEPTH, n_phys, PAGE_SIZE, N_HEADS, 2 * QK_HEAD_DIM), 1.0),
        "cache_v": nrm(ks[3], (DEPTH, n_phys, PAGE_SIZE, N_HEADS, V_HEAD_DIM), 1.0),
        "state_conv": nrm(ks[4], (DEPTH, DEC_BATCH, CONV_KERNEL - 1, CONV_WIDTH), 0.5),
        "state_lru_conv": nrm(ks[5], (DEPTH, DEC_BATCH, LRU_CONV - 1, LRU_WIDTH), 1.0),
        "state_lru_h": nrm(ks[6], (DEPTH, DEC_BATCH, LRU_WIDTH), 0.5),
        "page_table": jax.random.permutation(ks[7], n_phys)[: DEC_BATCH * n_pages]
                      .reshape(DEC_BATCH, n_pages).astype(jnp.int32),
        "norm_mix": gain(ks[8], (DEPTH, D_MODEL)),
        "w_in": nrm(ks[9], (DEPTH, D_MODEL, IN_COLS), D_MODEL ** -0.5),
        "q_norm": gain(ks[10], (DEPTH, QK_HEAD_DIM)),
        "k_norm": gain(ks[11], (DEPTH, QK_HEAD_DIM)),
        "lambda_q1": nrm(ks[12], (DEPTH, QK_HEAD_DIM), 0.1),
        "lambda_k1": nrm(ks[13], (DEPTH, QK_HEAD_DIM), 0.1),
        "lambda_q2": nrm(ks[14], (DEPTH, QK_HEAD_DIM), 0.1),
        "lambda_k2": nrm(ks[15], (DEPTH, QK_HEAD_DIM), 0.1),
        "subln": gain(ks[16], (DEPTH, V_HEAD_DIM)),
        "conv_w": nrm(ks[17], (DEPTH, CONV_KERNEL, CONV_WIDTH), CONV_KERNEL ** -0.5),
        "conv_b": nrm(ks[18], (DEPTH, CONV_WIDTH), 0.02),
        "conv_ln_g": gain(ks[19], (DEPTH, CONV_WIDTH)),
        "conv_ln_b": nrm(ks[20], (DEPTH, CONV_WIDTH), 0.02),
        "lru_conv_w": nrm(ks[21], (DEPTH, LRU_CONV, LRU_WIDTH), LRU_CONV ** -0.5),
        "lru_conv_b": nrm(ks[22], (DEPTH, LRU_WIDTH), 0.02),
        "lru_wa": nrm(ks[23], (DEPTH, LRU_HEADS, LRU_HEAD_DIM, LRU_HEAD_DIM), LRU_HEAD_DIM ** -0.5),
        "lru_ba": nrm(ks[24], (DEPTH, LRU_WIDTH), 0.02),
        "lru_wx": nrm(ks[25], (DEPTH, LRU_HEADS, LRU_HEAD_DIM, LRU_HEAD_DIM), LRU_HEAD_DIM ** -0.5),
        "lru_bx": nrm(ks[26], (DEPTH, LRU_WIDTH), 0.02),
        "lru_lambda": jnp.log(a_base) - jnp.log1p(-a_base),
        "w_out": nrm(ks[28], (DEPTH, MIX_WIDTH, D_MODEL), MIX_WIDTH ** -0.5),
        "norm_ffn": gain(ks[29], (DEPTH, D_MODEL)),
        "w_ffn_gate": nrm(ks[30], (DEPTH, D_MODEL, D_FF), D_MODEL ** -0.5),
        "w_ffn_up": nrm(ks[31], (DEPTH, D_MODEL, D_FF), D_MODEL ** -0.5),
        "w_ffn_down": nrm(ks[32], (DEPTH, D_FF, D_MODEL), D_FF ** -0.5),
    }


def reference(x_prompt, x_sample, cache_k, cache_v, state_conv, state_lru_conv, state_lru_h, page_table,
              norm_mix, w_in, q_norm, k_norm, lambda_q1, lambda_k1, lambda_q2, lambda_k2, subln,
              conv_w, conv_b, conv_ln_g, conv_ln_b, lru_conv_w, lru_conv_b, lru_wa, lru_ba, lru_wx, lru_bx,
              lru_lambda, w_out, norm_ffn, w_ffn_gate, w_ffn_up, w_ffn_down):
    params = dict(norm_mix=norm_mix, w_in=w_in, q_norm=q_norm, k_norm=k_norm,
                  lambda_q1=lambda_q1, lambda_k1=lambda_k1, lambda_q2=lambda_q2, lambda_k2=lambda_k2,
                  subln=subln, conv_w=conv_w, conv_b=conv_b, conv_ln_g=conv_ln_g, conv_ln_b=conv_ln_b,
                  lru_conv_w=lru_conv_w, lru_conv_b=lru_conv_b, lru_wa=lru_wa, lru_ba=lru_ba,
                  lru_wx=lru_wx, lru_bx=lru_bx, lru_lambda=lru_lambda, w_out=w_out, norm_ffn=norm_ffn,
                  w_ffn_gate=w_ffn_gate, w_ffn_up=w_ffn_up, w_ffn_down=w_ffn_down)
    y_prompt, k_prompt, v_prompt, conv_prompt, lru_conv_prompt, lru_h_prompt = trunk(
        x_prompt, 0, params, None)
    past_len = page_table.shape[1] * PAGE_SIZE
    y_sample, k_sample, v_sample, conv_sample, lru_conv_sample, lru_h_sample = trunk(
        x_sample, past_len, params,
        (cache_k, cache_v, state_conv, state_lru_conv, state_lru_h, page_table))
    return (y_prompt, y_sample, k_prompt, v_prompt, conv_prompt, lru_conv_prompt, lru_h_prompt,
            k_sample, v_sample, conv_sample, lru_conv_sample, lru_h_sample)
```

```python
import functools
import math

import jax
import jax.numpy as jnp
from jax import lax
from jax.experimental import pallas as pl
from jax.experimental.pallas import tpu as pltpu

F32 = jnp.float32
BF16 = jnp.bfloat16

D_MODEL = 2048
N_HEADS = 8
V_HEAD_DIM = 128
QK_HEAD_DIM = 64
ROPE_DIM = 16
ROPE_THETA = 500000.0
ATTN_SCALE = 1.0 / math.sqrt(QK_HEAD_DIM)
ATTN_WIDTH = N_HEADS * V_HEAD_DIM
CONV_WIDTH = 512
LRU_WIDTH = 512
CONV_KERNEL = 31
LRU_CONV = 4
LRU_C = 8.0
EPS = 1e-6
NEG_INF = -1e30
PAGE_SIZE = 128
PAGES_PER_STEP = 8
GROUP_COLS = 1024
VMEM_LIMIT = 56 * 1024 * 1024


def _cparams(sem):
    return pltpu.CompilerParams(dimension_semantics=sem, vmem_limit_bytes=VMEM_LIMIT)


def _sigmoid(x):
    return 1.0 / (1.0 + jnp.exp(-x))


def _silu(x):
    return x * _sigmoid(x)


def _gelu_tanh(x):
    c = math.sqrt(2.0 / math.pi)
    return x * (0.5 * (1.0 + jnp.tanh(c * (x + 0.044715 * (x * x * x)))))


def _log1p(z):
    w = 1.0 + z
    small = w == 1.0
    return jnp.where(small, z, jnp.log(w) * z / jnp.where(small, 1.0, w - 1.0))


def _expm1_nonpos(x):
    u = jnp.exp(x)
    direct = (u == 1.0) | (x < -20.0)
    ratio = (u - 1.0) * x / jnp.where(direct, 1.0, jnp.log(u))
    return jnp.where(u == 1.0, x, jnp.where(x < -20.0, u - 1.0, ratio))


def _lambda_full(lq1, lk1, lq2, lk2, lam_init):
    s1 = jnp.sum(lq1[...] * lk1[...], axis=-1, keepdims=True)
    s2 = jnp.sum(lq2[...] * lk2[...], axis=-1, keepdims=True)
    return jnp.exp(s1) - jnp.exp(s2) + lam_init


def _qk_post(y, gain, gmat, rc, rs1, rs2, scale):
    outs = []
    for h in range(N_HEADS):
        yh = y[:, h * 128:(h + 1) * 128]
        ms = jnp.dot((yh * yh).astype(BF16), gmat, preferred_element_type=F32)
        yn = yh * lax.rsqrt(ms + EPS) * gain
        rot = yn * rc + pltpu.roll(yn, 120, 1) * rs1 + pltpu.roll(yn, 8, 1) * rs2
        outs.append(rot * scale if scale != 1.0 else rot)
    return outs


def _in_proj_kernel(x_ref, g_ref, w_ref, qn_ref, kn_ref, gmat_ref, rc_ref, rs1_ref, rs2_ref,
                    q_ref, k_ref, kb_ref, v_ref, vb_ref, u_ref, gate_ref, xb_ref, xn_scr):
    j = pl.program_id(1)

    @pl.when(j == 0)
    def _():
        x = x_ref[...]
        ms = jnp.mean(x * x, axis=-1, keepdims=True)
        xn_scr[...] = (x * lax.rsqrt(ms + EPS) * g_ref[...]).astype(BF16)

    y = jnp.dot(xn_scr[...], w_ref[...], preferred_element_type=F32)

    @pl.when(j == 0)
    def _():
        outs = _qk_post(y, qn_ref[...], gmat_ref[...], rc_ref[...], rs1_ref[...], rs2_ref[...],
                        ATTN_SCALE)
        for h in range(N_HEADS):
            q_ref[:, h * 128:(h + 1) * 128] = outs[h].astype(BF16)

    @pl.when(j == 1)
    def _():
        outs = _qk_post(y, kn_ref[...], gmat_ref[...], rc_ref[...], rs1_ref[...], rs2_ref[...], 1.0)
        for h in range(N_HEADS):
            k_ref[:, h * 128:(h + 1) * 128] = outs[h]
            kb_ref[:, h * 128:(h + 1) * 128] = outs[h].astype(BF16)

    @pl.when(j == 2)
    def _():
        v_ref[...] = y
        vb_ref[...] = y.astype(BF16)

    @pl.when(j == 3)
    def _():
        u_ref[...] = y[:, :CONV_WIDTH] * _sigmoid(y[:, CONV_WIDTH:])

    @pl.when(j == 4)
    def _():
        gate_ref[...] = _gelu_tanh(y[:, :LRU_WIDTH])
        xb_ref[...] = y[:, LRU_WIDTH:]


def _in_proj(x, g, w, qn, kn, gmat, rc, rs1, rs2, tm):
    m = x.shape[0]
    nt = m // tm
    nrope = rc.shape[0] // tm
    row = lambda i, j: (i, 0)
    const = lambda i, j: (0, 0)
    rope = lambda i, j: (i % nrope, 0)
    out_shape = [
        jax.ShapeDtypeStruct((m, ATTN_WIDTH), BF16),
        jax.ShapeDtypeStruct((m, ATTN_WIDTH), F32),
        jax.ShapeDtypeStruct((m, ATTN_WIDTH), BF16),
        jax.ShapeDtypeStruct((m, ATTN_WIDTH), F32),
        jax.ShapeDtypeStruct((m, ATTN_WIDTH), BF16),
        jax.ShapeDtypeStruct((m, CONV_WIDTH), F32),
        jax.ShapeDtypeStruct((m, LRU_WIDTH), F32),
        jax.ShapeDtypeStruct((m, LRU_WIDTH), F32),
    ]
    out_specs = [
        pl.BlockSpec((tm, ATTN_WIDTH), row), pl.BlockSpec((tm, ATTN_WIDTH), row),
        pl.BlockSpec((tm, ATTN_WIDTH), row), pl.BlockSpec((tm, ATTN_WIDTH), row),
        pl.BlockSpec((tm, ATTN_WIDTH), row), pl.BlockSpec((tm, CONV_WIDTH), row),
        pl.BlockSpec((tm, LRU_WIDTH), row), pl.BlockSpec((tm, LRU_WIDTH), row),
    ]
    in_specs = [
        pl.BlockSpec((tm, D_MODEL), row),
        pl.BlockSpec((1, D_MODEL), const),
        pl.BlockSpec((D_MODEL, GROUP_COLS), lambda i, j: (0, j)),
        pl.BlockSpec((1, 128), const), pl.BlockSpec((1, 128), const),
        pl.BlockSpec((128, 128), const),
        pl.BlockSpec((tm, 128), rope), pl.BlockSpec((tm, 128), rope), pl.BlockSpec((tm, 128), rope),
    ]
    return pl.pallas_call(
        _in_proj_kernel,
        grid=(nt, 5),
        in_specs=in_specs,
        out_specs=out_specs,
        out_shape=out_shape,
        scratch_shapes=[pltpu.VMEM((tm, D_MODEL), BF16)],
        compiler_params=_cparams(("parallel", "arbitrary")),
        name="in_proj",
    )(x, g, w, qn, kn, gmat, rc, rs1, rs2)


def _attn_prompt_kernel(lq1, lk1, lq2, lk2, sub_ref, q_ref, k_ref, v_ref, o_ref, *,
                        lam_init, seq, tq):
    lam = _lambda_full(lq1, lk1, lq2, lk2, lam_init)
    lane = lax.broadcasted_iota(jnp.int32, (1, 128), 1)
    first = lane < QK_HEAD_DIM
    rows = lax.broadcasted_iota(jnp.int32, (2 * tq, tq), 0)
    cols = lax.broadcasted_iota(jnp.int32, (2 * tq, tq), 1)
    causal = cols <= jnp.where(rows >= tq, rows - tq, rows)
    nt_dims = (((1,), (1,)), ((), ()))

    def step(qs, k, v, carry, mask):
        m, l, acc = carry
        s = lax.dot_general(qs, k, nt_dims, preferred_element_type=F32)
        if mask is not None:
            s = jnp.where(mask, s, NEG_INF)
        m_new = jnp.maximum(m, jnp.max(s, axis=-1, keepdims=True))
        alpha = jnp.exp(m - m_new)
        p = jnp.exp(s - m_new)
        l = alpha * l + jnp.sum(p, axis=-1, keepdims=True)
        acc = alpha * acc + jnp.dot(p.astype(BF16), v, preferred_element_type=F32)
        return m_new, l, acc

    for qi in range(seq // tq):
        q = q_ref[qi * tq:(qi + 1) * tq, :]
        zero = jnp.zeros_like(q)
        qs = jnp.concatenate([jnp.where(first, q, zero), jnp.where(first, zero, q)], axis=0)
        carry = (jnp.full((2 * tq, 1), NEG_INF, F32), jnp.zeros((2 * tq, 1), F32),
                 jnp.zeros((2 * tq, V_HEAD_DIM), F32))

        def body(kj, c):
            start = pl.multiple_of(kj * tq, tq)
            return step(qs, k_ref[pl.ds(start, tq), :], v_ref[pl.ds(start, tq), :], c, None)

        if qi > 0:
            carry = lax.fori_loop(0, qi, body, carry)
        m, l, acc = step(qs, k_ref[qi * tq:(qi + 1) * tq, :], v_ref[qi * tq:(qi + 1) * tq, :],
                         carry, causal)
        on = acc / l
        o = on[:tq] - lam * on[tq:]
        ms = jnp.mean(o * o, axis=-1, keepdims=True)
        o = o * lax.rsqrt(ms + EPS) * sub_ref[...] * (1.0 - lam_init)
        o_ref[qi * tq:(qi + 1) * tq, :] = o.astype(o_ref.dtype)


def _attn_prompt(lams, sub, q, kb, vb, lam_init, batch, seq, tq=256):
    vec = pl.BlockSpec((1, QK_HEAD_DIM), lambda b, h: (0, 0))
    blk = pl.BlockSpec((seq, V_HEAD_DIM), lambda b, h: (b, h))
    return pl.pallas_call(
        functools.partial(_attn_prompt_kernel, lam_init=lam_init, seq=seq, tq=tq),
        grid=(batch, N_HEADS),
        in_specs=[vec, vec, vec, vec, pl.BlockSpec((1, V_HEAD_DIM), lambda b, h: (0, 0)),
                  blk, blk, blk],
        out_specs=blk,
        out_shape=jax.ShapeDtypeStruct((batch * seq, ATTN_WIDTH), BF16),
        compiler_params=_cparams(("parallel", "parallel")),
        name="attn_prompt",
    )(*lams, sub, q, kb, vb)


def _attn_sample_kernel(pt_ref, lq1, lk1, lq2, lk2, sub_ref, q_ref, kn_ref, vn_ref, *rest,
                        lam_init, n_new):
    g = PAGES_PER_STEP
    k_refs = rest[:g]
    v_refs = rest[g:2 * g]
    o_ref = rest[2 * g]
    m_scr, l_scr, acc_scr = rest[2 * g + 1:]
    step = pl.program_id(1)
    nrow = 2 * N_HEADS * n_new
    q = q_ref[...]
    nt_dims = (((1,), (1,)), ((), ()))

    def head_of_row(shape):
        r = lax.broadcasted_iota(jnp.int32, shape, 0)
        return (r % (N_HEADS * n_new)) // n_new, r % n_new

    def update(s, vmat):
        m = m_scr[...]
        m_new = jnp.maximum(m, jnp.max(s, axis=-1, keepdims=True))
        alpha = jnp.exp(m - m_new)
        p = jnp.exp(s - m_new)
        l_scr[...] = alpha * l_scr[...] + jnp.sum(p, axis=-1, keepdims=True)
        acc_scr[...] = alpha * acc_scr[...] + jnp.dot(p.astype(BF16), vmat,
                                                      preferred_element_type=F32)
        m_scr[...] = m_new

    @pl.when(step == 0)
    def _():
        m_scr[...] = jnp.full(m_scr.shape, NEG_INF, F32)
        l_scr[...] = jnp.zeros(l_scr.shape, F32)
        acc_scr[...] = jnp.zeros(acc_scr.shape, F32)
        kn = kn_ref[...].reshape(n_new * N_HEADS, V_HEAD_DIM).astype(BF16)
        vn = vn_ref[...].reshape(n_new * N_HEADS, V_HEAD_DIM).astype(BF16)
        s = lax.dot_general(q, kn, nt_dims, preferred_element_type=F32)
        shape = s.shape
        hrow, trow = head_of_row(shape)
        c = lax.broadcasted_iota(jnp.int32, shape, 1)
        ok = (hrow == c % N_HEADS) & (c // N_HEADS <= trow)
        update(jnp.where(ok, s, NEG_INF), vn)

    ncol = PAGE_SIZE * N_HEADS
    hrow, _ = head_of_row((nrow, ncol))
    ok = hrow == lax.broadcasted_iota(jnp.int32, (nrow, ncol), 1) % N_HEADS
    s_list = []
    m_new = m_scr[...]
    for p in range(g):
        kp = k_refs[p][...].reshape(ncol, V_HEAD_DIM).astype(BF16)
        s = lax.dot_general(q, kp, nt_dims, preferred_element_type=F32)
        s = jnp.where(ok, s, NEG_INF)
        s_list.append(s)
        m_new = jnp.maximum(m_new, jnp.max(s, axis=-1, keepdims=True))
    alpha = jnp.exp(m_scr[...] - m_new)
    l = alpha * l_scr[...]
    acc = alpha * acc_scr[...]
    for p in range(g):
        pr = jnp.exp(s_list[p] - m_new)
        l = l + jnp.sum(pr, axis=-1, keepdims=True)
        vp = v_refs[p][...].reshape(ncol, V_HEAD_DIM).astype(BF16)
        acc = acc + jnp.dot(pr.astype(BF16), vp, preferred_element_type=F32)
    m_scr[...] = m_new
    l_scr[...] = l
    acc_scr[...] = acc

    @pl.when(step == pl.num_programs(1) - 1)
    def _():
        lam = _lambda_full(lq1, lk1, lq2, lk2, lam_init)
        on = acc_scr[...] / l_scr[...]
        half = N_HEADS * n_new
        o = on[:half] - lam * on[half:]
        ms = jnp.mean(o * o, axis=-1, keepdims=True)
        o = o * lax.rsqrt(ms + EPS) * sub_ref[...] * (1.0 - lam_init)
        o_ref[...] = o.astype(o_ref.dtype)


def _attn_sample(page_table, lams, sub, q_rows, k_new, v_new, cache_k, cache_v, layer, lam_init):
    batch, n_pages = page_table.shape
    n_new = k_new.shape[1]
    g = PAGES_PER_STEP
    nrow = 2 * N_HEADS * n_new
    vec = pl.BlockSpec((1, QK_HEAD_DIM), lambda b, s, pt: (0, 0))
    new_spec = pl.BlockSpec((None, n_new, N_HEADS, V_HEAD_DIM), lambda b, s, pt: (b, 0, 0, 0))

    def page_spec(r):
        return pl.BlockSpec((None, None, PAGE_SIZE, N_HEADS, V_HEAD_DIM),
                            lambda b, s, pt: (layer, pt[b, s * g + r], 0, 0, 0))

    in_specs = ([vec, vec, vec, vec, pl.BlockSpec((1, V_HEAD_DIM), lambda b, s, pt: (0, 0)),
                 pl.BlockSpec((None, nrow, V_HEAD_DIM), lambda b, s, pt: (b, 0, 0)),
                 new_spec, new_spec]
                + [page_spec(r) for r in range(g)] + [page_spec(r) for r in range(g)])
    grid_spec = pltpu.PrefetchScalarGridSpec(
        num_scalar_prefetch=1,
        grid=(batch, n_pages // g),
        in_specs=in_specs,
        out_specs=pl.BlockSpec((None, N_HEADS * n_new, V_HEAD_DIM), lambda b, s, pt: (b, 0, 0)),
        scratch_shapes=[pltpu.VMEM((nrow, 1), F32), pltpu.VMEM((nrow, 1), F32),
                        pltpu.VMEM((nrow, V_HEAD_DIM), F32)],
    )
    return pl.pallas_call(
        functools.partial(_attn_sample_kernel, lam_init=lam_init, n_new=n_new),
        grid_spec=grid_spec,
        out_shape=jax.ShapeDtypeStruct((batch, N_HEADS * n_new, V_HEAD_DIM), BF16),
        compiler_params=_cparams(("parallel", "arbitrary")),
        name="attn_sample",
    )(page_table, *lams, sub, q_rows, k_new, v_new, *([cache_k] * g), *([cache_v] * g))


CONV_PAD = 32
CONV_CHUNK = 32


def _conv_kernel(u_ref, buf_ref, w_ref, b_ref, lg_ref, lb_ref, o_ref, nb_ref, pad_scr, *,
                 t_pad, t_real):
    pad_scr[0:CONV_PAD, :] = buf_ref[...]
    pad_scr[CONV_PAD:CONV_PAD + t_pad, :] = u_ref[...]
    off = CONV_PAD - (CONV_KERNEL - 1)
    rc = min(CONV_CHUNK, t_pad)

    def chunk(c0):
        acc = jnp.broadcast_to(b_ref[...], (rc, CONV_WIDTH))
        win = pad_scr[pl.ds(c0, rc + CONV_PAD), :]
        for s in range(8):
            taps = [j for j in range(CONV_KERNEL) if (off + j) % 8 == s]
            span = max(off + j for j in taps) - s + rc
            sh = win[s:s + span, :]
            for j in taps:
                a0 = off + j - s
                acc = acc + w_ref[j:j + 1, :] * sh[a0:a0 + rc, :]
        mu = jnp.mean(acc, axis=-1, keepdims=True)
        d = acc - mu
        var = jnp.mean(d * d, axis=-1, keepdims=True)
        y = d * lax.rsqrt(var + EPS) * lg_ref[...] + lb_ref[...]
        o_ref[pl.ds(c0, rc), :] = _silu(y).astype(o_ref.dtype)

    if t_pad == rc:
        chunk(0)
    else:
        def body(ci, carry):
            chunk(pl.multiple_of(ci * rc, rc))
            return carry
        lax.fori_loop(0, t_pad // rc, body, 0)
    nb_ref[...] = pad_scr[t_real:t_real + CONV_PAD, :]


def _conv_module(u, buf32, w, b, lg, lb, t_real):
    batch, t_pad, _ = u.shape
    vec = pl.BlockSpec((1, CONV_WIDTH), lambda i: (0, 0))
    return pl.pallas_call(
        functools.partial(_conv_kernel, t_pad=t_pad, t_real=t_real),
        grid=(batch,),
        in_specs=[pl.BlockSpec((None, t_pad, CONV_WIDTH), lambda i: (i, 0, 0)),
                  pl.BlockSpec((None, CONV_PAD, CONV_WIDTH), lambda i: (i, 0, 0)),
                  pl.BlockSpec((CONV_PAD, CONV_WIDTH), lambda i: (0, 0)),
                  vec, vec, vec],
        out_specs=[pl.BlockSpec((None, t_pad, CONV_WIDTH), lambda i: (i, 0, 0)),
                   pl.BlockSpec((None, CONV_PAD, CONV_WIDTH), lambda i: (i, 0, 0))],
        out_shape=[jax.ShapeDtypeStruct((batch, t_pad, CONV_WIDTH), BF16),
                   jax.ShapeDtypeStruct((batch, CONV_PAD, CONV_WIDTH), F32)],
        scratch_shapes=[pltpu.VMEM((CONV_PAD + t_pad, CONV_WIDTH), F32)],
        compiler_params=_cparams(("parallel",)),
        name="conv_module",
    )(u, buf32, w, b, lg, lb)


LRU_PAD = 8
LRU_CHUNK = 256


def _lru_kernel(gate_ref, xb_ref, buf_ref, h0_ref, cw_ref, cb_ref, wa_ref, ba_ref, wx_ref, bx_ref,
                lam_ref, y_ref, nb_ref, hl_ref, pad_scr, a_scr, b_scr, h_scr, *, t_pad, t_real):
    pad_scr[0:LRU_PAD, :] = buf_ref[...]
    pad_scr[LRU_PAD:LRU_PAD + t_pad, :] = xb_ref[...]
    off = LRU_PAD - (LRU_CONV - 1)
    rc = min(LRU_CHUNK, t_pad)
    lam = lam_ref[...]
    sp = jnp.maximum(-lam, 0.0) + _log1p(jnp.exp(-jnp.abs(lam)))

    def gates(c0):
        xc = jnp.broadcast_to(cb_ref[...], (rc, LRU_WIDTH))
        win = pad_scr[pl.ds(c0, rc + LRU_PAD), :]
        for j in range(LRU_CONV):
            xc = xc + cw_ref[j:j + 1, :] * win[off + j:off + j + rc, :]
        xcb = xc.astype(BF16)
        r = _sigmoid(jnp.dot(xcb, wa_ref[...], preferred_element_type=F32) + ba_ref[...])
        i = _sigmoid(jnp.dot(xcb, wx_ref[...], preferred_element_type=F32) + bx_ref[...])
        log_a = (-LRU_C) * r * sp
        a_scr[pl.ds(c0, rc), :] = jnp.exp(log_a)
        b_scr[pl.ds(c0, rc), :] = jnp.sqrt(-_expm1_nonpos(2.0 * log_a)) * (i * xc)

    if t_pad == rc:
        gates(0)
    else:
        def gbody(ci, carry):
            gates(pl.multiple_of(ci * rc, rc))
            return carry
        lax.fori_loop(0, t_pad // rc, gbody, 0)

    row = lax.broadcasted_iota(jnp.int32, (8, LRU_WIDTH), 0)

    def sbody(gi, h):
        r0 = pl.multiple_of(gi * 8, 8)
        a = a_scr[pl.ds(r0, 8), :]
        b = b_scr[pl.ds(r0, 8), :]
        for s in (1, 2, 4):
            a_sh = jnp.where(row >= s, pltpu.roll(a, s, 0), 1.0)
            b_sh = jnp.where(row >= s, pltpu.roll(b, s, 0), 0.0)
            b = a * b_sh + b
            a = a * a_sh
        hs = a * h + b
        h_scr[pl.ds(r0, 8), :] = hs
        return hs[7:8, :]

    lax.fori_loop(0, t_pad // 8, sbody, h0_ref[...])
    y_ref[...] = (h_scr[...] * gate_ref[...]).astype(y_ref.dtype)
    nb_ref[...] = pad_scr[t_real:t_real + LRU_PAD, :]
    hl_ref[...] = h_scr[t_real - 1:t_real, :]


def _lru_block(gate, xb, buf8, h0, cw, cb, wa, ba, wx, bx, lam, t_real):
    batch, t_pad, _ = gate.shape
    vec = pl.BlockSpec((1, LRU_WIDTH), lambda i: (0, 0))
    seq = pl.BlockSpec((None, t_pad, LRU_WIDTH), lambda i: (i, 0, 0))
    mat = pl.BlockSpec((LRU_WIDTH, LRU_WIDTH), lambda i: (0, 0))
    return pl.pallas_call(
        functools.partial(_lru_kernel, t_pad=t_pad, t_real=t_real),
        grid=(batch,),
        in_specs=[seq, seq,
                  pl.BlockSpec((None, LRU_PAD, LRU_WIDTH), lambda i: (i, 0, 0)),
                  pl.BlockSpec((None, 1, LRU_WIDTH), lambda i: (i, 0, 0)),
                  pl.BlockSpec((LRU_CONV, LRU_WIDTH), lambda i: (0, 0)),
                  vec, mat, vec, mat, vec, vec],
        out_specs=[seq,
                   pl.BlockSpec((None, LRU_PAD, LRU_WIDTH), lambda i: (i, 0, 0)),
                   pl.BlockSpec((None, 1, LRU_WIDTH), lambda i: (i, 0, 0))],
        out_shape=[jax.ShapeDtypeStruct((batch, t_pad, LRU_WIDTH), BF16),
                   jax.ShapeDtypeStruct((batch, LRU_PAD, LRU_WIDTH), F32),
                   jax.ShapeDtypeStruct((batch, 1, LRU_WIDTH), F32)],
        scratch_shapes=[pltpu.VMEM((LRU_PAD + t_pad, LRU_WIDTH), F32),
                        pltpu.VMEM((t_pad, LRU_WIDTH), F32),
                        pltpu.VMEM((t_pad, LRU_WIDTH), F32),
                        pltpu.VMEM((t_pad, LRU_WIDTH), F32)],
        compiler_params=_cparams(("parallel",)),
        name="lru_block",
    )(gate, xb, buf8, h0, cw, cb, wa, ba, wx, bx, lam)


def _out_proj_kernel(x_ref, oa_ref, oc_ref, ol_ref, w_ref, y_ref):
    acc = x_ref[...]
    acc = acc + jnp.dot(oa_ref[...], w_ref[0:ATTN_WIDTH, :], preferred_element_type=F32)
    acc = acc + jnp.dot(oc_ref[...], w_ref[ATTN_WIDTH:ATTN_WIDTH + CONV_WIDTH, :],
                        preferred_element_type=F32)
    acc = acc + jnp.dot(ol_ref[...], w_ref[ATTN_WIDTH + CONV_WIDTH:, :],
                        preferred_element_type=F32)
    y_ref[...] = acc


def _out_proj(x, oa, oc, ol, w, tm):
    m = x.shape[0]
    row = lambda i: (i, 0)
    return pl.pallas_call(
        _out_proj_kernel,
        grid=(m // tm,),
        in_specs=[pl.BlockSpec((tm, D_MODEL), row), pl.BlockSpec((tm, ATTN_WIDTH), row),
                  pl.BlockSpec((tm, CONV_WIDTH), row), pl.BlockSpec((tm, LRU_WIDTH), row),
                  pl.BlockSpec((D_MODEL, D_MODEL), lambda i: (0, 0))],
        out_specs=pl.BlockSpec((tm, D_MODEL), row),
        out_shape=jax.ShapeDtypeStruct((m, D_MODEL), F32),
        compiler_params=_cparams(("parallel",)),
        name="out_proj",
    )(x, oa, oc, ol, w)


def _ffn_kernel(x_ref, g_ref, wg_ref, wu_ref, wd_ref, y_ref, xn_scr):
    f = pl.program_id(1)

    @pl.when(f == 0)
    def _():
        x = x_ref[...]
        ms = jnp.mean(x * x, axis=-1, keepdims=True)
        xn_scr[...] = (x * lax.rsqrt(ms + EPS) * g_ref[...]).astype(BF16)
        y_ref[...] = x

    xn = xn_scr[...]
    gate = jnp.dot(xn, wg_ref[...], preferred_element_type=F32)
    up = jnp.dot(xn, wu_ref[...], preferred_element_type=F32)
    hid = (_silu(gate) * up).astype(BF16)
    y_ref[...] += jnp.dot(hid, wd_ref[...], preferred_element_type=F32)


def _ffn(x, g, wg, wu, wd, tm, tf):
    m = x.shape[0]
    d_ff = wg.shape[1]
    row = lambda i, f: (i, 0)
    return pl.pallas_call(
        _ffn_kernel,
        grid=(m // tm, d_ff // tf),
        in_specs=[pl.BlockSpec((tm, D_MODEL), row),
                  pl.BlockSpec((1, D_MODEL), lambda i, f: (0, 0)),
                  pl.BlockSpec((D_MODEL, tf), lambda i, f: (0, f)),
                  pl.BlockSpec((D_MODEL, tf), lambda i, f: (0, f)),
                  pl.BlockSpec((tf, D_MODEL), lambda i, f: (f, 0))],
        out_specs=pl.BlockSpec((tm, D_MODEL), row),
        out_shape=jax.ShapeDtypeStruct((m, D_MODEL), F32),
        scratch_shapes=[pltpu.VMEM((tm, D_MODEL), BF16)],
        compiler_params=_cparams(("parallel", "arbitrary")),
        name="ffn",
    )(x, g, wg, wu, wd)


def _rope_tables(pos):
    half = ROPE_DIM // 2
    inv_freq = ROPE_THETA ** (-jnp.arange(0, ROPE_DIM, 2, dtype=F32) / ROPE_DIM)
    ang = pos.astype(F32)[:, None] * inv_freq[None, :]
    cos, sin = jnp.cos(ang), jnp.sin(ang)
    t = pos.shape[0]
    ones = jnp.ones((t, QK_HEAD_DIM - ROPE_DIM), F32)
    zeros = jnp.zeros((t, QK_HEAD_DIM - ROPE_DIM), F32)
    zh = jnp.zeros((t, half), F32)
    rc = jnp.concatenate([cos, cos, ones], axis=1)
    rs1 = jnp.concatenate([-sin, zh, zeros], axis=1)
    rs2 = jnp.concatenate([zh, sin, zeros], axis=1)
    return tuple(jnp.concatenate([a, a], axis=1) for a in (rc, rs1, rs2))


def _block_diag(w):
    h, di, dj = w.shape
    eye = jnp.eye(h, dtype=w.dtype)
    return (eye[:, None, :, None] * w[:, :, None, :]).reshape(h * di, h * dj)


def _group_mean_matrix():
    r = jnp.arange(128) // QK_HEAD_DIM
    return ((r[:, None] == r[None, :]).astype(F32) / QK_HEAD_DIM).astype(BF16)


def _layer(x, p, l, rope, tm, tf, batch, t_real, past):
    lam_init = 0.8 - 0.6 * math.exp(-0.3 * l)
    m = x.shape[0]
    row = lambda a: a[l].reshape(1, -1)
    tile2 = lambda a: jnp.concatenate([a[l], a[l]]).reshape(1, 128)
    lams = (row(p['lambda_q1']), row(p['lambda_k1']), row(p['lambda_q2']), row(p['lambda_k2']))
    sub = row(p['subln'])

    q, k, kb, v, vb, u, gate, xb = _in_proj(
        x, row(p['norm_mix']), p['w_in_bf'][l], tile2(p['q_norm']), tile2(p['k_norm']),
        p['gmat'], *rope, tm=tm)

    if past is None:
        o_attn = _attn_prompt(lams, sub, q, kb, vb, lam_init, batch, t_real)
        conv_buf = jnp.zeros((batch, CONV_PAD, CONV_WIDTH), F32)
        lru_buf = jnp.zeros((batch, LRU_PAD, LRU_WIDTH), F32)
        h0 = jnp.zeros((batch, 1, LRU_WIDTH), F32)
        t_pad = t_real
    else:
        cache_k, cache_v, state_conv, state_lru_conv, state_lru_h, page_table = past
        q5 = q.reshape(batch, t_real, N_HEADS, 2, QK_HEAD_DIM)
        zero = jnp.zeros_like(q5[:, :, :, 0])
        q_rows = jnp.stack([jnp.concatenate([q5[:, :, :, 0], zero], axis=-1),
                            jnp.concatenate([zero, q5[:, :, :, 1]], axis=-1)], axis=1)
        q_rows = q_rows.transpose(0, 1, 3, 2, 4).reshape(batch, 2 * N_HEADS * t_real, V_HEAD_DIM)
        o = _attn_sample(page_table, lams, sub, q_rows,
                         k.reshape(batch, t_real, N_HEADS, V_HEAD_DIM),
                         v.reshape(batch, t_real, N_HEADS, V_HEAD_DIM),
                         cache_k, cache_v, l, lam_init)
        o_attn = (o.reshape(batch, N_HEADS, t_real, V_HEAD_DIM).transpose(0, 2, 1, 3)
                  .reshape(m, ATTN_WIDTH))
        conv_buf = jnp.pad(state_conv[l], ((0, 0), (CONV_PAD - (CONV_KERNEL - 1), 0), (0, 0)))
        lru_buf = jnp.pad(state_lru_conv[l], ((0, 0), (LRU_PAD - (LRU_CONV - 1), 0), (0, 0)))
        h0 = state_lru_h[l].reshape(batch, 1, LRU_WIDTH)
        t_pad = -(-t_real // 8) * 8

    def seq3(a):
        a = a.reshape(batch, t_real, a.shape[-1])
        return a if t_pad == t_real else jnp.pad(a, ((0, 0), (0, t_pad - t_real), (0, 0)))

    conv_w = jnp.pad(p['conv_w'][l], ((0, CONV_PAD - CONV_KERNEL), (0, 0)))
    o_conv, nb_conv = _conv_module(seq3(u), conv_buf, conv_w, row(p['conv_b']),
                                   row(p['conv_ln_g']), row(p['conv_ln_b']), t_real)
    o_lru, nb_lru, h_last = _lru_block(
        seq3(gate), seq3(xb), lru_buf, h0, p['lru_conv_w'][l], row(p['lru_conv_b']),
        p['lru_wa_bd'][l], row(p['lru_ba']), p['lru_wx_bd'][l], row(p['lru_bx']),
        row(p['lru_lambda']), t_real)
    o_conv = o_conv[:, :t_real].reshape(m, CONV_WIDTH)
    o_lru = o_lru[:, :t_real].reshape(m, LRU_WIDTH)

    x = _out_proj(x, o_attn, o_conv, o_lru, p['w_out_bf'][l], tm)
    x = _ffn(x, row(p['norm_ffn']), p['w_gate_bf'][l], p['w_up_bf'][l], p['w_down_bf'][l], tm, tf)
    states = (k.reshape(batch, t_real, N_HEADS, 2 * QK_HEAD_DIM),
              v.reshape(batch, t_real, N_HEADS, V_HEAD_DIM),
              nb_conv[:, CONV_PAD - (CONV_KERNEL - 1):],
              nb_lru[:, LRU_PAD - (LRU_CONV - 1):],
              h_last.reshape(batch, LRU_WIDTH))
    return x, states


def _trunk(x, start_pos, p, past, tm, tf):
    batch, t, _ = x.shape
    depth = p['w_in_bf'].shape[0]
    pos = start_pos + jnp.arange(t, dtype=jnp.int32)
    rope = _rope_tables(pos)
    if t < tm:
        rope = tuple(jnp.tile(a, (tm // t, 1)) for a in rope)
    x = x.reshape(batch * t, D_MODEL)
    outs = []
    for l in range(depth):
        x, st = _layer(x, p, l, rope, tm, tf, batch, t, past)
        outs.append(st)
    stacked = tuple(jnp.stack([o[i] for o in outs]) for i in range(5))
    return (x.reshape(batch, t, D_MODEL),) + stacked


def kernel(x_prompt, x_sample, cache_k, cache_v, state_conv, state_lru_conv, state_lru_h, page_table,
           norm_mix, w_in, q_norm, k_norm, lambda_q1, lambda_k1, lambda_q2, lambda_k2, subln,
           conv_w, conv_b, conv_ln_g, conv_ln_b, lru_conv_w, lru_conv_b, lru_wa, lru_ba, lru_wx, lru_bx,
           lru_lambda, w_out, norm_ffn, w_ffn_gate, w_ffn_up, w_ffn_down):
    p = dict(norm_mix=norm_mix, q_norm=q_norm, k_norm=k_norm,
             lambda_q1=lambda_q1, lambda_k1=lambda_k1, lambda_q2=lambda_q2, lambda_k2=lambda_k2,
             subln=subln, conv_w=conv_w, conv_b=conv_b, conv_ln_g=conv_ln_g, conv_ln_b=conv_ln_b,
             lru_conv_w=lru_conv_w, lru_conv_b=lru_conv_b, lru_ba=lru_ba, lru_bx=lru_bx,
             lru_lambda=lru_lambda, norm_ffn=norm_ffn)
    p['w_in_bf'] = w_in.astype(BF16)
    p['w_out_bf'] = w_out.astype(BF16)
    p['w_gate_bf'] = w_ffn_gate.astype(BF16)
    p['w_up_bf'] = w_ffn_up.astype(BF16)
    p['w_down_bf'] = w_ffn_down.astype(BF16)
    p['lru_wa_bd'] = jax.vmap(_block_diag)(lru_wa).astype(BF16)
    p['lru_wx_bd'] = jax.vmap(_block_diag)(lru_wx).astype(BF16)
    p['gmat'] = _group_mean_matrix()

    y_p, k_p, v_p, cb_p, lcb_p, h_p = _trunk(x_prompt, 0, p, None, tm=512, tf=512)
    past_len = page_table.shape[1] * PAGE_SIZE
    y_s, k_s, v_s, cb_s, lcb_s, h_s = _trunk(
        x_sample, past_len, p,
        (cache_k, cache_v, state_conv, state_lru_conv, state_lru_h, page_table), tm=128, tf=512)
    return (y_p, y_s, k_p, v_p, cb_p, lcb_p, h_p, k_s, v_s, cb_s, lcb_s, h_s)
```

```python
import functools
import math

import jax
import jax.numpy as jnp
from jax import lax
from jax.experimental import pallas as pl
from jax.experimental.pallas import tpu as pltpu

F32 = jnp.float32
BF16 = jnp.bfloat16

D_MODEL = 2048
N_HEADS = 8
V_HEAD_DIM = 128
QK_HEAD_DIM = 64
ROPE_DIM = 16
ROPE_THETA = 500000.0
ATTN_SCALE = 1.0 / math.sqrt(QK_HEAD_DIM)
LOG2_E = 1.4426950408889634
Q_SCALE = ATTN_SCALE * LOG2_E
ATTN_WIDTH = N_HEADS * V_HEAD_DIM
CONV_WIDTH = 512
LRU_WIDTH = 512
CONV_KERNEL = 31
LRU_CONV = 4
LRU_C = 8.0
EPS = 1e-6
NEG_INF = -1e30
PAGE_SIZE = 128
PAGES_PER_STEP = 8
GROUP_COLS = 1024
MXU_COLS = 256
BF16_ROWS = 16
VMEM_LIMIT = 56 * 1024 * 1024


def _tiles(m):
    return min(512, m), 512


def _in_proj_tile(m):
    return min(256, m)


def _cparams(sem):
    return pltpu.CompilerParams(dimension_semantics=sem, vmem_limit_bytes=VMEM_LIMIT)


def _sigmoid(x):
    return 1.0 / (1.0 + jnp.exp(-x))


def _silu(x):
    return x * _sigmoid(x)


def _gelu_tanh(x):
    c = math.sqrt(2.0 / math.pi)
    return x * (0.5 * (1.0 + jnp.tanh(c * (x + 0.044715 * (x * x * x)))))


def _log1p(z):
    w = 1.0 + z
    small = w == 1.0
    return jnp.where(small, z, jnp.log(w) * z / jnp.where(small, 1.0, w - 1.0))


def _expm1_nonpos(x):
    u = jnp.exp(x)
    direct = (u == 1.0) | (x < -20.0)
    ratio = (u - 1.0) * x / jnp.where(direct, 1.0, jnp.log(u))
    return jnp.where(u == 1.0, x, jnp.where(x < -20.0, u - 1.0, ratio))


def _lambda_full(lq1, lk1, lq2, lk2, lam_init):
    s1 = jnp.sum(lq1[...] * lk1[...], axis=-1, keepdims=True)
    s2 = jnp.sum(lq2[...] * lk2[...], axis=-1, keepdims=True)
    return jnp.exp(s1) - jnp.exp(s2) + lam_init


def _qk_chunk(y, gain, gmat, rc, rs1, rs2, scale):
    ms = jnp.dot((y * y).astype(BF16), gmat, preferred_element_type=F32)
    yn = y * lax.rsqrt(ms + EPS) * gain
    rot = yn * rc + pltpu.roll(yn, MXU_COLS - ROPE_DIM // 2, 1) * rs1 \
        + pltpu.roll(yn, ROPE_DIM // 2, 1) * rs2
    return rot * scale if scale != 1.0 else rot


def _in_proj_kernel(x_ref, g_ref, w_ref, qn_ref, kn_ref, gmat_ref, rc_ref, rs1_ref, rs2_ref,
                    q_ref, k_ref, kb_ref, v_ref, vb_ref, u_ref, gate_ref, xb_ref, xn_scr):
    x = x_ref[...]
    ms = jnp.mean(x * x, axis=-1, keepdims=True)
    xn_scr[...] = (x * lax.rsqrt(ms + EPS) * g_ref[...]).astype(BF16)

    def cols(c):
        return slice(c * MXU_COLS, (c + 1) * MXU_COLS)

    def proj(group):
        w = w_ref[:, group * GROUP_COLS:(group + 1) * GROUP_COLS]
        return jnp.dot(xn_scr[...], w, preferred_element_type=F32)

    y = proj(0)
    for c in range(GROUP_COLS // MXU_COLS):
        r = _qk_chunk(y[:, cols(c)], qn_ref[...], gmat_ref[...], rc_ref[...], rs1_ref[...],
                      rs2_ref[...], Q_SCALE)
        q_ref[:, cols(c)] = r.astype(BF16)

    y = proj(1)
    for c in range(GROUP_COLS // MXU_COLS):
        r = _qk_chunk(y[:, cols(c)], kn_ref[...], gmat_ref[...], rc_ref[...], rs1_ref[...],
                      rs2_ref[...], 1.0)
        k_ref[:, cols(c)] = r
        kb_ref[:, cols(c)] = r.astype(BF16)

    y = proj(2)
    v_ref[...] = y
    vb_ref[...] = y.astype(BF16)

    y = proj(3)
    u_ref[...] = y[:, :CONV_WIDTH] * _sigmoid(y[:, CONV_WIDTH:])

    y = proj(4)
    gate_ref[...] = _gelu_tanh(y[:, :LRU_WIDTH])
    xb_ref[...] = y[:, LRU_WIDTH:]


def _in_proj(x, g, w, layer, qn, kn, gmat, rc, rs1, rs2):
    m = x.shape[0]
    tm = _in_proj_tile(m)
    nt = m // tm
    nrope = rc.shape[0] // tm
    row = lambda i: (i, 0)
    const = lambda i: (0, 0)
    rope = lambda i: (i % nrope, 0)
    out_shape = [
        jax.ShapeDtypeStruct((m, ATTN_WIDTH), BF16),
        jax.ShapeDtypeStruct((m, ATTN_WIDTH), F32),
        jax.ShapeDtypeStruct((m, ATTN_WIDTH), BF16),
        jax.ShapeDtypeStruct((m, ATTN_WIDTH), F32),
        jax.ShapeDtypeStruct((m, ATTN_WIDTH), BF16),
        jax.ShapeDtypeStruct((m, CONV_WIDTH), F32),
        jax.ShapeDtypeStruct((m, LRU_WIDTH), F32),
        jax.ShapeDtypeStruct((m, LRU_WIDTH), F32),
    ]
    out_specs = [
        pl.BlockSpec((tm, ATTN_WIDTH), row), pl.BlockSpec((tm, ATTN_WIDTH), row),
        pl.BlockSpec((tm, ATTN_WIDTH), row), pl.BlockSpec((tm, ATTN_WIDTH), row),
        pl.BlockSpec((tm, ATTN_WIDTH), row), pl.BlockSpec((tm, CONV_WIDTH), row),
        pl.BlockSpec((tm, LRU_WIDTH), row), pl.BlockSpec((tm, LRU_WIDTH), row),
    ]
    in_specs = [
        pl.BlockSpec((tm, D_MODEL), row),
        pl.BlockSpec((1, D_MODEL), const),
        pl.BlockSpec((None, D_MODEL, 5 * GROUP_COLS), lambda i: (layer, 0, 0),
                     pipeline_mode=pl.Buffered(1)),
        pl.BlockSpec((1, MXU_COLS), const), pl.BlockSpec((1, MXU_COLS), const),
        pl.BlockSpec((MXU_COLS, MXU_COLS), const),
        pl.BlockSpec((tm, MXU_COLS), rope), pl.BlockSpec((tm, MXU_COLS), rope),
        pl.BlockSpec((tm, MXU_COLS), rope),
    ]
    return pl.pallas_call(
        _in_proj_kernel,
        grid=(nt,),
        in_specs=in_specs,
        out_specs=out_specs,
        out_shape=out_shape,
        scratch_shapes=[pltpu.VMEM((tm, D_MODEL), BF16)],
        compiler_params=_cparams(("parallel",)),
        name="in_proj",
    )(x, g, w, qn, kn, gmat, rc, rs1, rs2)


HEADS_PER_STEP = 2


def _attn_prompt_kernel(lq1, lk1, lq2, lk2, sub_ref, q_ref, k_ref, v_ref, o_ref,
                        qt_scr, vt_scr, m_scr, acc_scr, *, lam_init, seq, tq):
    lam = _lambda_full(lq1, lk1, lq2, lk2, lam_init)
    heads = range(HEADS_PER_STEP)
    hcols = lambda h: slice(h * V_HEAD_DIM, (h + 1) * V_HEAD_DIM)
    for c in range(seq // tq):
        sl = slice(c * tq, (c + 1) * tq)
        qt_scr[:, sl] = q_ref[sl, :].astype(F32).T.astype(BF16)
        vt = v_ref[sl, :].astype(F32).T.astype(BF16)
        for h in heads:
            vt_scr[h, 0:V_HEAD_DIM, sl] = vt[hcols(h)]
    first_row = lax.broadcasted_iota(jnp.int32, (BF16_ROWS, seq), 0) == 0
    for h in heads:
        vt_scr[h, V_HEAD_DIM:V_HEAD_DIM + BF16_ROWS, :] = \
            jnp.where(first_row, 1.0, 0.0).astype(BF16)

    key = lax.broadcasted_iota(jnp.int32, (tq, 2 * tq), 0)
    col = lax.broadcasted_iota(jnp.int32, (tq, 2 * tq), 1)
    causal = key <= jnp.where(col >= tq, col - tq, col)
    zeros = jnp.zeros((QK_HEAD_DIM, tq), BF16)

    def step(qs, start, masked):
        s = [jnp.dot(k_ref[pl.ds(start, tq), hcols(h)], qs[h], preferred_element_type=F32)
             for h in heads]
        if masked:
            s = [jnp.where(causal, sh, NEG_INF) for sh in s]
        m_new = [jnp.maximum(m_scr[h], jnp.max(s[h], axis=0, keepdims=True)) for h in heads]
        for h in heads:
            alpha = jnp.exp2(m_scr[h] - m_new[h])
            p = jnp.exp2(s[h] - m_new[h]).astype(BF16)
            pv = jnp.dot(vt_scr[h, :, pl.ds(start, tq)], p, preferred_element_type=F32)
            acc_scr[h] = alpha * acc_scr[h] + pv
            m_scr[h] = m_new[h]

    for qi in range(seq // tq):
        qs = []
        for h in heads:
            qt = qt_scr[hcols(h), qi * tq:(qi + 1) * tq]
            qs.append(jnp.concatenate(
                [jnp.concatenate([qt[:QK_HEAD_DIM], zeros], axis=0),
                 jnp.concatenate([zeros, qt[QK_HEAD_DIM:]], axis=0)], axis=1))
        m_scr[...] = jnp.full(m_scr.shape, NEG_INF, F32)
        acc_scr[...] = jnp.zeros(acc_scr.shape, F32)

        if qi > 0:
            def body(kj, carry):
                step(qs, pl.multiple_of(kj * tq, tq), False)
                return carry
            lax.fori_loop(0, qi, body, 0, unroll=2)
        step(qs, qi * tq, True)

        for h in heads:
            acc = acc_scr[h]
            on = acc[0:V_HEAD_DIM] / acc[V_HEAD_DIM:V_HEAD_DIM + 1]
            d = on[:, :tq] - lam * on[:, tq:]
            ms = jnp.mean(d * d, axis=0, keepdims=True)
            y = d * lax.rsqrt(ms + EPS) * sub_ref[...] * (1.0 - lam_init)
            o_ref[qi * tq:(qi + 1) * tq, hcols(h)] = y.T.astype(o_ref.dtype)


def _attn_prompt(lams, sub, q, kb, vb, lam_init, batch, seq, tq=256):
    hp = HEADS_PER_STEP
    vec = pl.BlockSpec((1, QK_HEAD_DIM), lambda b, h: (0, 0))
    blk = pl.BlockSpec((seq, hp * V_HEAD_DIM), lambda b, h: (b, h))
    sub_cols = jnp.broadcast_to(sub.reshape(V_HEAD_DIM, 1), (V_HEAD_DIM, tq))
    return pl.pallas_call(
        functools.partial(_attn_prompt_kernel, lam_init=lam_init, seq=seq, tq=tq),
        grid=(batch, N_HEADS // hp),
        in_specs=[vec, vec, vec, vec, pl.BlockSpec((V_HEAD_DIM, tq), lambda b, h: (0, 0)),
                  blk, blk, blk],
        out_specs=blk,
        out_shape=jax.ShapeDtypeStruct((batch * seq, ATTN_WIDTH), BF16),
        scratch_shapes=[pltpu.VMEM((hp * V_HEAD_DIM, seq), BF16),
                        pltpu.VMEM((hp, V_HEAD_DIM + BF16_ROWS, seq), BF16),
                        pltpu.VMEM((hp, 1, 2 * tq), F32),
                        pltpu.VMEM((hp, V_HEAD_DIM + BF16_ROWS, 2 * tq), F32)],
        compiler_params=_cparams(("parallel", "parallel")),
        name="attn_prompt",
    )(*lams, sub_cols, q, kb, vb)


def _attn_sample_kernel(pt_ref, lq1, lk1, lq2, lk2, sub_ref, q_ref, kn_ref, vn_ref, *rest,
                        lam_init, n_new):
    g = PAGES_PER_STEP
    k_refs = rest[:g]
    v_refs = rest[g:2 * g]
    o_ref = rest[2 * g]
    m_scr, l_scr, acc_scr = rest[2 * g + 1:]
    step = pl.program_id(1)
    nrow = 2 * N_HEADS * n_new
    q = q_ref[...]
    nt_dims = (((1,), (1,)), ((), ()))

    def head_of_row(shape):
        r = lax.broadcasted_iota(jnp.int32, shape, 0)
        return (r % (N_HEADS * n_new)) // n_new, r % n_new

    def update(s, vmat):
        m = m_scr[...]
        m_new = jnp.maximum(m, jnp.max(s, axis=-1, keepdims=True))
        alpha = jnp.exp2(m - m_new)
        p = jnp.exp2(s - m_new)
        l_scr[...] = alpha * l_scr[...] + jnp.sum(p, axis=-1, keepdims=True)
        acc_scr[...] = alpha * acc_scr[...] + jnp.dot(p.astype(BF16), vmat,
                                                      preferred_element_type=F32)
        m_scr[...] = m_new

    @pl.when(step == 0)
    def _():
        m_scr[...] = jnp.full(m_scr.shape, NEG_INF, F32)
        l_scr[...] = jnp.zeros(l_scr.shape, F32)
        acc_scr[...] = jnp.zeros(acc_scr.shape, F32)
        kn = kn_ref[...].reshape(n_new * N_HEADS, V_HEAD_DIM).astype(BF16)
        vn = vn_ref[...].reshape(n_new * N_HEADS, V_HEAD_DIM).astype(BF16)
        s = lax.dot_general(q, kn, nt_dims, preferred_element_type=F32)
        shape = s.shape
        hrow, trow = head_of_row(shape)
        c = lax.broadcasted_iota(jnp.int32, shape, 1)
        ok = (hrow == c % N_HEADS) & (c // N_HEADS <= trow)
        update(jnp.where(ok, s, NEG_INF), vn)

    ncol = PAGE_SIZE * N_HEADS
    hrow, _ = head_of_row((nrow, ncol))
    ok = hrow == lax.broadcasted_iota(jnp.int32, (nrow, ncol), 1) % N_HEADS
    s_list = []
    m_new = m_scr[...]
    for p in range(g):
        kp = k_refs[p][...].reshape(ncol, V_HEAD_DIM).astype(BF16)
        s = lax.dot_general(q, kp, nt_dims, preferred_element_type=F32)
        s = jnp.where(ok, s, NEG_INF)
        s_list.append(s)
        m_new = jnp.maximum(m_new, jnp.max(s, axis=-1, keepdims=True))
    alpha = jnp.exp2(m_scr[...] - m_new)
    l = alpha * l_scr[...]
    acc = alpha * acc_scr[...]
    for p in range(g):
        pr = jnp.exp2(s_list[p] - m_new)
        l = l + jnp.sum(pr, axis=-1, keepdims=True)
        vp = v_refs[p][...].reshape(ncol, V_HEAD_DIM).astype(BF16)
        acc = acc + jnp.dot(pr.astype(BF16), vp, preferred_element_type=F32)
    m_scr[...] = m_new
    l_scr[...] = l
    acc_scr[...] = acc

    @pl.when(step == pl.num_programs(1) - 1)
    def _():
        lam = _lambda_full(lq1, lk1, lq2, lk2, lam_init)
        on = acc_scr[...] / l_scr[...]
        half = N_HEADS * n_new
        o = on[:half] - lam * on[half:]
        ms = jnp.mean(o * o, axis=-1, keepdims=True)
        o = o * lax.rsqrt(ms + EPS) * sub_ref[...] * (1.0 - lam_init)
        o_ref[...] = o.astype(o_ref.dtype)


def _attn_sample(page_table, lams, sub, q_rows, k_new, v_new, cache_k, cache_v, layer, lam_init):
    batch, n_pages = page_table.shape
    n_new = k_new.shape[1]
    g = PAGES_PER_STEP
    nrow = 2 * N_HEADS * n_new
    vec = pl.BlockSpec((1, QK_HEAD_DIM), lambda b, s, pt: (0, 0))
    new_spec = pl.BlockSpec((None, n_new, N_HEADS, V_HEAD_DIM), lambda b, s, pt: (b, 0, 0, 0))

    def page_spec(r):
        return pl.BlockSpec((None, None, PAGE_SIZE, N_HEADS, V_HEAD_DIM),
                            lambda b, s, pt: (layer, pt[b, s * g + r], 0, 0, 0))

    in_specs = ([vec, vec, vec, vec, pl.BlockSpec((1, V_HEAD_DIM), lambda b, s, pt: (0, 0)),
                 pl.BlockSpec((None, nrow, V_HEAD_DIM), lambda b, s, pt: (b, 0, 0)),
                 new_spec, new_spec]
                + [page_spec(r) for r in range(g)] + [page_spec(r) for r in range(g)])
    grid_spec = pltpu.PrefetchScalarGridSpec(
        num_scalar_prefetch=1,
        grid=(batch, n_pages // g),
        in_specs=in_specs,
        out_specs=pl.BlockSpec((None, N_HEADS * n_new, V_HEAD_DIM), lambda b, s, pt: (b, 0, 0)),
        scratch_shapes=[pltpu.VMEM((nrow, 1), F32), pltpu.VMEM((nrow, 1), F32),
                        pltpu.VMEM((nrow, V_HEAD_DIM), F32)],
    )
    return pl.pallas_call(
        functools.partial(_attn_sample_kernel, lam_init=lam_init, n_new=n_new),
        grid_spec=grid_spec,
        out_shape=jax.ShapeDtypeStruct((batch, N_HEADS * n_new, V_HEAD_DIM), BF16),
        compiler_params=_cparams(("parallel", "arbitrary")),
        name="attn_sample",
    )(page_table, *lams, sub, q_rows, k_new, v_new, *([cache_k] * g), *([cache_v] * g))


CONV_PAD = 32
CONV_CHUNK = 32
CONV_UNROLL = 4
SUBLANES = 8


def _shift_matrix(nwin):
    span = nwin - SUBLANES
    r = jnp.arange((SUBLANES - 1) * span)
    src = r % span + r // span + 1
    one = (src[:, None] == jnp.arange(nwin)[None, :]).astype(BF16)
    return jnp.concatenate([one, one, one], axis=1)


def _conv_kernel(u_ref, buf_ref, w_ref, b_ref, lg_ref, lb_ref, shift_ref, o_ref, nb_ref, pad_scr,
                 *, t_pad, t_real):
    pad_scr[0:CONV_PAD, :] = buf_ref[...]
    pad_scr[CONV_PAD:CONV_PAD + t_pad, :] = u_ref[...]
    off = CONV_PAD - (CONV_KERNEL - 1)
    rc = min(CONV_CHUNK, t_pad)
    nwin = rc + CONV_PAD
    span = nwin - SUBLANES

    def chunk(c0):
        acc = jnp.broadcast_to(b_ref[...], (rc, CONV_WIDTH))
        win = pad_scr[pl.ds(c0, nwin), :]
        hi = win.astype(BF16)
        r1 = win - hi.astype(F32)
        mid = r1.astype(BF16)
        lo = (r1 - mid.astype(F32)).astype(BF16)
        shifted = jnp.dot(shift_ref[...], jnp.concatenate([hi, mid, lo], axis=0),
                          preferred_element_type=F32)
        for s in range(SUBLANES):
            taps = [j for j in range(CONV_KERNEL) if (off + j) % SUBLANES == s]
            sh = win if s == 0 else shifted[(s - 1) * span:s * span]
            for j in taps:
                a0 = off + j - s
                acc = acc + w_ref[j:j + 1, :] * sh[a0:a0 + rc, :]
        mu = jnp.mean(acc, axis=-1, keepdims=True)
        d = acc - mu
        var = jnp.mean(d * d, axis=-1, keepdims=True)
        y = d * lax.rsqrt(var + EPS) * lg_ref[...] + lb_ref[...]
        o_ref[pl.ds(c0, rc), :] = _silu(y).astype(o_ref.dtype)

    if t_pad == rc:
        chunk(0)
    else:
        def body(ci, carry):
            chunk(pl.multiple_of(ci * rc, rc))
            return carry
        lax.fori_loop(0, t_pad // rc, body, 0, unroll=CONV_UNROLL)
    nb_ref[...] = pad_scr[t_real:t_real + CONV_PAD, :]


def _conv_module(u, buf32, w, b, lg, lb, t_real):
    batch, t_pad, _ = u.shape
    vec = pl.BlockSpec((1, CONV_WIDTH), lambda i: (0, 0))
    shift = _shift_matrix(min(CONV_CHUNK, t_pad) + CONV_PAD)
    return pl.pallas_call(
        functools.partial(_conv_kernel, t_pad=t_pad, t_real=t_real),
        grid=(batch,),
        in_specs=[pl.BlockSpec((None, t_pad, CONV_WIDTH), lambda i: (i, 0, 0)),
                  pl.BlockSpec((None, CONV_PAD, CONV_WIDTH), lambda i: (i, 0, 0)),
                  pl.BlockSpec((CONV_PAD, CONV_WIDTH), lambda i: (0, 0)),
                  vec, vec, vec, pl.BlockSpec(shift.shape, lambda i: (0, 0))],
        out_specs=[pl.BlockSpec((None, t_pad, CONV_WIDTH), lambda i: (i, 0, 0)),
                   pl.BlockSpec((None, CONV_PAD, CONV_WIDTH), lambda i: (i, 0, 0))],
        out_shape=[jax.ShapeDtypeStruct((batch, t_pad, CONV_WIDTH), BF16),
                   jax.ShapeDtypeStruct((batch, CONV_PAD, CONV_WIDTH), F32)],
        scratch_shapes=[pltpu.VMEM((CONV_PAD + t_pad, CONV_WIDTH), F32)],
        compiler_params=_cparams(("parallel",)),
        name="conv_module",
    )(u, buf32, w, b, lg, lb, shift)


LRU_PAD = 8
LRU_CHUNK = 256


def _lru_kernel(gate_ref, xb_ref, buf_ref, h0_ref, cw_ref, cb_ref, wa_ref, ba_ref, wx_ref, bx_ref,
                lam_ref, y_ref, nb_ref, hl_ref, pad_scr, a_scr, b_scr, h_scr, *, t_pad, t_real):
    pad_scr[0:LRU_PAD, :] = buf_ref[...]
    pad_scr[LRU_PAD:LRU_PAD + t_pad, :] = xb_ref[...]
    off = LRU_PAD - (LRU_CONV - 1)
    rc = min(LRU_CHUNK, t_pad)
    lam = lam_ref[...]
    sp = jnp.maximum(-lam, 0.0) + _log1p(jnp.exp(-jnp.abs(lam)))

    def gates(c0):
        xc = jnp.broadcast_to(cb_ref[...], (rc, LRU_WIDTH))
        win = pad_scr[pl.ds(c0, rc + LRU_PAD), :]
        for j in range(LRU_CONV):
            xc = xc + cw_ref[j:j + 1, :] * win[off + j:off + j + rc, :]
        xcb = xc.astype(BF16)
        r = _sigmoid(jnp.dot(xcb, wa_ref[...], preferred_element_type=F32) + ba_ref[...])
        i = _sigmoid(jnp.dot(xcb, wx_ref[...], preferred_element_type=F32) + bx_ref[...])
        log_a = (-LRU_C) * r * sp
        a_scr[pl.ds(c0, rc), :] = jnp.exp(log_a)
        b_scr[pl.ds(c0, rc), :] = jnp.sqrt(-_expm1_nonpos(2.0 * log_a)) * (i * xc)

    if t_pad == rc:
        gates(0)
    else:
        def gbody(ci, carry):
            gates(pl.multiple_of(ci * rc, rc))
            return carry
        lax.fori_loop(0, t_pad // rc, gbody, 0)

    row = lax.broadcasted_iota(jnp.int32, (8, LRU_WIDTH), 0)

    def sbody(gi, h):
        r0 = pl.multiple_of(gi * 8, 8)
        a = a_scr[pl.ds(r0, 8), :]
        b = b_scr[pl.ds(r0, 8), :]
        for s in (1, 2, 4):
            a_sh = jnp.where(row >= s, pltpu.roll(a, s, 0), 1.0)
            b_sh = jnp.where(row >= s, pltpu.roll(b, s, 0), 0.0)
            b = a * b_sh + b
            a = a * a_sh
        hs = a * h + b
        h_scr[pl.ds(r0, 8), :] = hs
        return hs[7:8, :]

    lax.fori_loop(0, t_pad // 8, sbody, h0_ref[...])
    y_ref[...] = (h_scr[...] * gate_ref[...]).astype(y_ref.dtype)
    nb_ref[...] = pad_scr[t_real:t_real + LRU_PAD, :]
    hl_ref[...] = h_scr[t_real - 1:t_real, :]


def _lru_block(gate, xb, buf8, h0, cw, cb, wa, ba, wx, bx, lam, layer, t_real):
    batch, t_pad, _ = gate.shape
    vec = pl.BlockSpec((1, LRU_WIDTH), lambda i: (0, 0))
    seq = pl.BlockSpec((None, t_pad, LRU_WIDTH), lambda i: (i, 0, 0))
    mat = pl.BlockSpec((None, LRU_WIDTH, LRU_WIDTH), lambda i: (layer, 0, 0))
    return pl.pallas_call(
        functools.partial(_lru_kernel, t_pad=t_pad, t_real=t_real),
        grid=(batch,),
        in_specs=[seq, seq,
                  pl.BlockSpec((None, LRU_PAD, LRU_WIDTH), lambda i: (i, 0, 0)),
                  pl.BlockSpec((None, 1, LRU_WIDTH), lambda i: (i, 0, 0)),
                  pl.BlockSpec((LRU_CONV, LRU_WIDTH), lambda i: (0, 0)),
                  vec, mat, vec, mat, vec, vec],
        out_specs=[seq,
                   pl.BlockSpec((None, LRU_PAD, LRU_WIDTH), lambda i: (i, 0, 0)),
                   pl.BlockSpec((None, 1, LRU_WIDTH), lambda i: (i, 0, 0))],
        out_shape=[jax.ShapeDtypeStruct((batch, t_pad, LRU_WIDTH), BF16),
                   jax.ShapeDtypeStruct((batch, LRU_PAD, LRU_WIDTH), F32),
                   jax.ShapeDtypeStruct((batch, 1, LRU_WIDTH), F32)],
        scratch_shapes=[pltpu.VMEM((LRU_PAD + t_pad, LRU_WIDTH), F32),
                        pltpu.VMEM((t_pad, LRU_WIDTH), F32),
                        pltpu.VMEM((t_pad, LRU_WIDTH), F32),
                        pltpu.VMEM((t_pad, LRU_WIDTH), F32)],
        compiler_params=_cparams(("parallel",)),
        name="lru_block",
    )(gate, xb, buf8, h0, cw, cb, wa, ba, wx, bx, lam)


def _out_proj_kernel(x_ref, oa_ref, oc_ref, ol_ref, w_ref, y_ref):
    acc = x_ref[...]
    acc = acc + jnp.dot(oa_ref[...], w_ref[0:ATTN_WIDTH, :], preferred_element_type=F32)
    acc = acc + jnp.dot(oc_ref[...], w_ref[ATTN_WIDTH:ATTN_WIDTH + CONV_WIDTH, :],
                        preferred_element_type=F32)
    acc = acc + jnp.dot(ol_ref[...], w_ref[ATTN_WIDTH + CONV_WIDTH:, :],
                        preferred_element_type=F32)
    y_ref[...] = acc


def _out_proj(x, oa, oc, ol, w, layer):
    m = x.shape[0]
    tm, _ = _tiles(m)
    row = lambda i: (i, 0)
    return pl.pallas_call(
        _out_proj_kernel,
        grid=(m // tm,),
        in_specs=[pl.BlockSpec((tm, D_MODEL), row), pl.BlockSpec((tm, ATTN_WIDTH), row),
                  pl.BlockSpec((tm, CONV_WIDTH), row), pl.BlockSpec((tm, LRU_WIDTH), row),
                  pl.BlockSpec((None, D_MODEL, D_MODEL), lambda i: (layer, 0, 0))],
        out_specs=pl.BlockSpec((tm, D_MODEL), row),
        out_shape=jax.ShapeDtypeStruct((m, D_MODEL), F32),
        compiler_params=_cparams(("parallel",)),
        name="out_proj",
    )(x, oa, oc, ol, w)


def _ffn_kernel(x_ref, g_ref, wg_ref, wu_ref, wd_ref, y_ref, xn_scr):
    f = pl.program_id(1)

    @pl.when(f == 0)
    def _():
        x = x_ref[...]
        ms = jnp.mean(x * x, axis=-1, keepdims=True)
        xn_scr[...] = (x * lax.rsqrt(ms + EPS) * g_ref[...]).astype(BF16)
        y_ref[...] = x

    xn = xn_scr[...]
    gate = jnp.dot(xn, wg_ref[...], preferred_element_type=F32)
    up = jnp.dot(xn, wu_ref[...], preferred_element_type=F32)
    hid = (_silu(gate) * up).astype(BF16)
    y_ref[...] += jnp.dot(hid, wd_ref[...], preferred_element_type=F32)


def _ffn(x, g, wg, wu, wd, layer):
    m = x.shape[0]
    tm, tf = _tiles(m)
    d_ff = wg.shape[-1]
    row = lambda i, f: (i, 0)
    return pl.pallas_call(
        _ffn_kernel,
        grid=(m // tm, d_ff // tf),
        in_specs=[pl.BlockSpec((tm, D_MODEL), row),
                  pl.BlockSpec((1, D_MODEL), lambda i, f: (0, 0)),
                  pl.BlockSpec((None, D_MODEL, tf), lambda i, f: (layer, 0, f)),
                  pl.BlockSpec((None, D_MODEL, tf), lambda i, f: (layer, 0, f)),
                  pl.BlockSpec((None, tf, D_MODEL), lambda i, f: (layer, f, 0))],
        out_specs=pl.BlockSpec((tm, D_MODEL), row),
        out_shape=jax.ShapeDtypeStruct((m, D_MODEL), F32),
        scratch_shapes=[pltpu.VMEM((tm, D_MODEL), BF16)],
        compiler_params=_cparams(("parallel", "arbitrary")),
        name="ffn",
    )(x, g, wg, wu, wd)


def _rope_tables(pos):
    half = ROPE_DIM // 2
    inv_freq = ROPE_THETA ** (-jnp.arange(0, ROPE_DIM, 2, dtype=F32) / ROPE_DIM)
    ang = pos.astype(F32)[:, None] * inv_freq[None, :]
    cos, sin = jnp.cos(ang), jnp.sin(ang)
    t = pos.shape[0]
    ones = jnp.ones((t, QK_HEAD_DIM - ROPE_DIM), F32)
    zeros = jnp.zeros((t, QK_HEAD_DIM - ROPE_DIM), F32)
    zh = jnp.zeros((t, half), F32)
    rc = jnp.concatenate([cos, cos, ones], axis=1)
    rs1 = jnp.concatenate([-sin, zh, zeros], axis=1)
    rs2 = jnp.concatenate([zh, sin, zeros], axis=1)
    return tuple(jnp.tile(a, (1, MXU_COLS // QK_HEAD_DIM)) for a in (rc, rs1, rs2))


def _block_diag(w):
    h, di, dj = w.shape
    eye = jnp.eye(h, dtype=w.dtype)
    return (eye[:, None, :, None] * w[:, :, None, :]).reshape(h * di, h * dj)


def _group_mean_matrix():
    r = jnp.arange(MXU_COLS) // QK_HEAD_DIM
    return ((r[:, None] == r[None, :]).astype(F32) / QK_HEAD_DIM).astype(BF16)


def _layer(x, p, l, rope, batch, t_real, past):
    lam_init = 0.8 - 0.6 * math.exp(-0.3 * l)
    m = x.shape[0]
    row = lambda a: a[l].reshape(1, -1)
    gain = lambda a: jnp.tile(a[l], MXU_COLS // QK_HEAD_DIM).reshape(1, MXU_COLS)
    lams = (row(p['lambda_q1']), row(p['lambda_k1']), row(p['lambda_q2']), row(p['lambda_k2']))
    sub = row(p['subln'])

    q, k, kb, v, vb, u, gate, xb = _in_proj(
        x, row(p['norm_mix']), p['w_in_bf'], l, gain(p['q_norm']), gain(p['k_norm']),
        p['gmat'], *rope)

    if past is None:
        o_attn = _attn_prompt(lams, sub, q, kb, vb, lam_init, batch, t_real)
        conv_buf = jnp.zeros((batch, CONV_PAD, CONV_WIDTH), F32)
        lru_buf = jnp.zeros((batch, LRU_PAD, LRU_WIDTH), F32)
        h0 = jnp.zeros((batch, 1, LRU_WIDTH), F32)
    else:
        cache_k, cache_v, state_conv, state_lru_conv, state_lru_h, page_table = past
        q5 = q.reshape(batch, t_real, N_HEADS, 2, QK_HEAD_DIM)
        zero = jnp.zeros_like(q5[:, :, :, 0])
        q_rows = jnp.stack([jnp.concatenate([q5[:, :, :, 0], zero], axis=-1),
                            jnp.concatenate([zero, q5[:, :, :, 1]], axis=-1)], axis=1)
        q_rows = q_rows.transpose(0, 1, 3, 2, 4).reshape(batch, 2 * N_HEADS * t_real, V_HEAD_DIM)
        o = _attn_sample(page_table, lams, sub, q_rows,
                         k.reshape(batch, t_real, N_HEADS, V_HEAD_DIM),
                         v.reshape(batch, t_real, N_HEADS, V_HEAD_DIM),
                         cache_k, cache_v, l, lam_init)
        o_attn = (o.reshape(batch, N_HEADS, t_real, V_HEAD_DIM).transpose(0, 2, 1, 3)
                  .reshape(m, ATTN_WIDTH))
        conv_buf = jnp.pad(state_conv[l], ((0, 0), (CONV_PAD - (CONV_KERNEL - 1), 0), (0, 0)))
        lru_buf = jnp.pad(state_lru_conv[l], ((0, 0), (LRU_PAD - (LRU_CONV - 1), 0), (0, 0)))
        h0 = state_lru_h[l].reshape(batch, 1, LRU_WIDTH)

    def seq3(a, rows):
        a = a.reshape(batch, t_real, a.shape[-1])
        pad = -t_real % rows
        return a if pad == 0 else jnp.pad(a, ((0, 0), (0, pad), (0, 0)))

    conv_w = jnp.pad(p['conv_w'][l], ((0, CONV_PAD - CONV_KERNEL), (0, 0)))
    o_conv, nb_conv = _conv_module(seq3(u, BF16_ROWS), conv_buf, conv_w, row(p['conv_b']),
                                   row(p['conv_ln_g']), row(p['conv_ln_b']), t_real)
    o_lru, nb_lru, h_last = _lru_block(
        seq3(gate, SUBLANES), seq3(xb, SUBLANES), lru_buf, h0, p['lru_conv_w'][l],
        row(p['lru_conv_b']),
        p['lru_wa_bd'], row(p['lru_ba']), p['lru_wx_bd'], row(p['lru_bx']),
        row(p['lru_lambda']), l, t_real)
    o_conv = o_conv[:, :t_real].reshape(m, CONV_WIDTH)
    o_lru = o_lru[:, :t_real].reshape(m, LRU_WIDTH)

    x = _out_proj(x, o_attn, o_conv, o_lru, p['w_out_bf'], l)
    x = _ffn(x, row(p['norm_ffn']), p['w_gate_bf'], p['w_up_bf'], p['w_down_bf'], l)
    states = (k.reshape(batch, t_real, N_HEADS, 2 * QK_HEAD_DIM),
              v.reshape(batch, t_real, N_HEADS, V_HEAD_DIM),
              nb_conv[:, CONV_PAD - (CONV_KERNEL - 1):],
              nb_lru[:, LRU_PAD - (LRU_CONV - 1):],
              h_last.reshape(batch, LRU_WIDTH))
    return x, states


def _trunk(x, start_pos, p, past):
    batch, t, _ = x.shape
    depth = p['w_in_bf'].shape[0]
    tm = _in_proj_tile(batch * t)
    pos = start_pos + jnp.arange(t, dtype=jnp.int32)
    rope = _rope_tables(pos)
    if t < tm:
        rope = tuple(jnp.tile(a, (tm // t, 1)) for a in rope)
    x = x.reshape(batch * t, D_MODEL)
    outs = []
    for l in range(depth):
        x, st = _layer(x, p, l, rope, batch, t, past)
        outs.append(st)
    stacked = tuple(jnp.stack([o[i] for o in outs]) for i in range(5))
    return (x.reshape(batch, t, D_MODEL),) + stacked


def kernel(x_prompt, x_sample, cache_k, cache_v, state_conv, state_lru_conv, state_lru_h, page_table,
           norm_mix, w_in, q_norm, k_norm, lambda_q1, lambda_k1, lambda_q2, lambda_k2, subln,
           conv_w, conv_b, conv_ln_g, conv_ln_b, lru_conv_w, lru_conv_b, lru_wa, lru_ba, lru_wx, lru_bx,
           lru_lambda, w_out, norm_ffn, w_ffn_gate, w_ffn_up, w_ffn_down):
    p = dict(norm_mix=norm_mix, q_norm=q_norm, k_norm=k_norm,
             lambda_q1=lambda_q1, lambda_k1=lambda_k1, lambda_q2=lambda_q2, lambda_k2=lambda_k2,
             subln=subln, conv_w=conv_w, conv_b=conv_b, conv_ln_g=conv_ln_g, conv_ln_b=conv_ln_b,
             lru_conv_w=lru_conv_w, lru_conv_b=lru_conv_b, lru_ba=lru_ba, lru_bx=lru_bx,
             lru_lambda=lru_lambda, norm_ffn=norm_ffn)
    p['w_in_bf'] = w_in.astype(BF16)
    p['w_out_bf'] = w_out.astype(BF16)
    p['w_gate_bf'] = w_ffn_gate.astype(BF16)
    p['w_up_bf'] = w_ffn_up.astype(BF16)
    p['w_down_bf'] = w_ffn_down.astype(BF16)
    p['lru_wa_bd'] = jax.vmap(_block_diag)(lru_wa).astype(BF16)
    p['lru_wx_bd'] = jax.vmap(_block_diag)(lru_wx).astype(BF16)
    p['gmat'] = _group_mean_matrix()

    y_p, k_p, v_p, cb_p, lcb_p, h_p = _trunk(x_prompt, 0, p, None)
    past_len = page_table.shape[1] * PAGE_SIZE
    y_s, k_s, v_s, cb_s, lcb_s, h_s = _trunk(
        x_sample, past_len, p,
        (cache_k, cache_v, state_conv, state_lru_conv, state_lru_h, page_table))
    return (y_p, y_s, k_p, v_p, cb_p, lcb_p, h_p, k_s, v_s, cb_s, lcb_s, h_s)
```

```python
import functools
import math

import jax
import jax.numpy as jnp
from jax import lax
from jax.experimental import pallas as pl
from jax.experimental.pallas import tpu as pltpu

F32 = jnp.float32
BF16 = jnp.bfloat16

D_MODEL = 2048
N_HEADS = 8
V_HEAD_DIM = 128
QK_HEAD_DIM = 64
ROPE_DIM = 16
ROPE_THETA = 500000.0
ATTN_SCALE = 1.0 / math.sqrt(QK_HEAD_DIM)
LOG2_E = 1.4426950408889634
Q_SCALE = ATTN_SCALE * LOG2_E
ATTN_WIDTH = N_HEADS * V_HEAD_DIM
CONV_WIDTH = 512
LRU_WIDTH = 512
CONV_KERNEL = 31
LRU_CONV = 4
LRU_C = 8.0
EPS = 1e-6
NEG_INF = -1e30
PAGE_SIZE = 128
PAGES_PER_STEP = 8
GROUP_COLS = 1024
MXU_COLS = 256
BF16_ROWS = 16
VMEM_LIMIT = 56 * 1024 * 1024


def _tiles(m):
    return min(512, m), 512


def _in_proj_tile(m):
    return min(256, m)


def _cparams(sem):
    return pltpu.CompilerParams(dimension_semantics=sem, vmem_limit_bytes=VMEM_LIMIT)


def _sigmoid(x):
    return 1.0 / (1.0 + jnp.exp(-x))


def _silu(x):
    return x * _sigmoid(x)


def _gelu_tanh(x):
    c = math.sqrt(2.0 / math.pi)
    return x * (0.5 * (1.0 + jnp.tanh(c * (x + 0.044715 * (x * x * x)))))


def _log1p(z):
    w = 1.0 + z
    small = w == 1.0
    return jnp.where(small, z, jnp.log(w) * z / jnp.where(small, 1.0, w - 1.0))


def _expm1_nonpos(x):
    u = jnp.exp(x)
    direct = (u == 1.0) | (x < -20.0)
    ratio = (u - 1.0) * x / jnp.where(direct, 1.0, jnp.log(u))
    return jnp.where(u == 1.0, x, jnp.where(x < -20.0, u - 1.0, ratio))


def _lambda_full(lq1, lk1, lq2, lk2, lam_init):
    s1 = jnp.sum(lq1[...] * lk1[...], axis=-1, keepdims=True)
    s2 = jnp.sum(lq2[...] * lk2[...], axis=-1, keepdims=True)
    return jnp.exp(s1) - jnp.exp(s2) + lam_init


def _qk_chunk(y, gain, gmat, rc, rs1, rs2, scale):
    ms = jnp.dot((y * y).astype(BF16), gmat, preferred_element_type=F32)
    yn = y * lax.rsqrt(ms + EPS) * gain
    rot = yn * rc + pltpu.roll(yn, MXU_COLS - ROPE_DIM // 2, 1) * rs1 \
        + pltpu.roll(yn, ROPE_DIM // 2, 1) * rs2
    return rot * scale if scale != 1.0 else rot


def _in_proj_kernel(x_ref, g_ref, w_ref, qn_ref, kn_ref, gmat_ref, rc_ref, rs1_ref, rs2_ref,
                    q_ref, k_ref, kb_ref, v_ref, vb_ref, u_ref, gate_ref, xb_ref, xn_scr):
    x = x_ref[...]
    ms = jnp.mean(x * x, axis=-1, keepdims=True)
    xn_scr[...] = (x * lax.rsqrt(ms + EPS) * g_ref[...]).astype(BF16)

    def cols(c):
        return slice(c * MXU_COLS, (c + 1) * MXU_COLS)

    def proj(group):
        w = w_ref[:, group * GROUP_COLS:(group + 1) * GROUP_COLS]
        return jnp.dot(xn_scr[...], w, preferred_element_type=F32)

    y = proj(0)
    for c in range(GROUP_COLS // MXU_COLS):
        r = _qk_chunk(y[:, cols(c)], qn_ref[...], gmat_ref[...], rc_ref[...], rs1_ref[...],
                      rs2_ref[...], Q_SCALE)
        q_ref[:, cols(c)] = r.astype(BF16)

    y = proj(1)
    for c in range(GROUP_COLS // MXU_COLS):
        r = _qk_chunk(y[:, cols(c)], kn_ref[...], gmat_ref[...], rc_ref[...], rs1_ref[...],
                      rs2_ref[...], 1.0)
        k_ref[:, cols(c)] = r
        kb_ref[:, cols(c)] = r.astype(BF16)

    y = proj(2)
    v_ref[...] = y
    vb_ref[...] = y.astype(BF16)

    y = proj(3)
    u_ref[...] = y[:, :CONV_WIDTH] * _sigmoid(y[:, CONV_WIDTH:])

    y = proj(4)
    gate_ref[...] = _gelu_tanh(y[:, :LRU_WIDTH])
    xb_ref[...] = y[:, LRU_WIDTH:]


def _in_proj(x, g, w, layer, qn, kn, gmat, rc, rs1, rs2):
    m = x.shape[0]
    tm = _in_proj_tile(m)
    assert m % tm == 0 and rc.shape[0] % tm == 0
    nt = m // tm
    nrope = rc.shape[0] // tm
    row = lambda i: (i, 0)
    const = lambda i: (0, 0)
    rope = lambda i: (i % nrope, 0)
    out_shape = [
        jax.ShapeDtypeStruct((m, ATTN_WIDTH), BF16),
        jax.ShapeDtypeStruct((m, ATTN_WIDTH), F32),
        jax.ShapeDtypeStruct((m, ATTN_WIDTH), BF16),
        jax.ShapeDtypeStruct((m, ATTN_WIDTH), F32),
        jax.ShapeDtypeStruct((m, ATTN_WIDTH), BF16),
        jax.ShapeDtypeStruct((m, CONV_WIDTH), F32),
        jax.ShapeDtypeStruct((m, LRU_WIDTH), F32),
        jax.ShapeDtypeStruct((m, LRU_WIDTH), F32),
    ]
    out_specs = [
        pl.BlockSpec((tm, ATTN_WIDTH), row), pl.BlockSpec((tm, ATTN_WIDTH), row),
        pl.BlockSpec((tm, ATTN_WIDTH), row), pl.BlockSpec((tm, ATTN_WIDTH), row),
        pl.BlockSpec((tm, ATTN_WIDTH), row), pl.BlockSpec((tm, CONV_WIDTH), row),
        pl.BlockSpec((tm, LRU_WIDTH), row), pl.BlockSpec((tm, LRU_WIDTH), row),
    ]
    in_specs = [
        pl.BlockSpec((tm, D_MODEL), row),
        pl.BlockSpec((1, D_MODEL), const),
        pl.BlockSpec((None, D_MODEL, 5 * GROUP_COLS), lambda i: (layer, 0, 0),
                     pipeline_mode=pl.Buffered(1)),
        pl.BlockSpec((1, MXU_COLS), const), pl.BlockSpec((1, MXU_COLS), const),
        pl.BlockSpec((MXU_COLS, MXU_COLS), const),
        pl.BlockSpec((tm, MXU_COLS), rope), pl.BlockSpec((tm, MXU_COLS), rope),
        pl.BlockSpec((tm, MXU_COLS), rope),
    ]
    return pl.pallas_call(
        _in_proj_kernel,
        grid=(nt,),
        in_specs=in_specs,
        out_specs=out_specs,
        out_shape=out_shape,
        scratch_shapes=[pltpu.VMEM((tm, D_MODEL), BF16)],
        compiler_params=_cparams(("parallel",)),
        name="in_proj",
    )(x, g, w, qn, kn, gmat, rc, rs1, rs2)


HEADS_PER_STEP = 2


def _attn_prompt_kernel(lq1, lk1, lq2, lk2, sub_ref, q_ref, k_ref, v_ref, o_ref,
                        qt_scr, vt_scr, m_scr, acc_scr, sa_scr, sb_scr, *, lam_init, seq, tq):
    lam = _lambda_full(lq1, lk1, lq2, lk2, lam_init)
    heads = range(HEADS_PER_STEP)
    hcols = lambda h: slice(h * V_HEAD_DIM, (h + 1) * V_HEAD_DIM)
    for c in range(seq // tq):
        sl = slice(c * tq, (c + 1) * tq)
        qt_scr[:, sl] = q_ref[sl, :].astype(F32).T.astype(BF16)
        vt = v_ref[sl, :].astype(F32).T.astype(BF16)
        for h in heads:
            vt_scr[h, 0:V_HEAD_DIM, sl] = vt[hcols(h)]
    first_row = lax.broadcasted_iota(jnp.int32, (BF16_ROWS, seq), 0) == 0
    for h in heads:
        vt_scr[h, V_HEAD_DIM:V_HEAD_DIM + BF16_ROWS, :] = \
            jnp.where(first_row, 1.0, 0.0).astype(BF16)

    key = lax.broadcasted_iota(jnp.int32, (tq, 2 * tq), 0)
    col = lax.broadcasted_iota(jnp.int32, (tq, 2 * tq), 1)
    causal = key <= jnp.where(col >= tq, col - tq, col)
    zeros = jnp.zeros((QK_HEAD_DIM, tq), BF16)

    bufs = (sa_scr, sb_scr)

    def produce(qs, buf, start):
        for h in heads:
            buf[h] = jnp.dot(k_ref[pl.ds(start, tq), hcols(h)], qs[h],
                             preferred_element_type=F32)

    def consume(buf, start, masked):
        s = [buf[h] for h in heads]
        if masked:
            s = [jnp.where(causal, sh, NEG_INF) for sh in s]
        m_new = [jnp.maximum(m_scr[h], jnp.max(s[h], axis=0, keepdims=True)) for h in heads]
        for h in heads:
            alpha = jnp.exp2(m_scr[h] - m_new[h])
            p = jnp.exp2(s[h] - m_new[h]).astype(BF16)
            pv = jnp.dot(vt_scr[h, :, pl.ds(start, tq)], p, preferred_element_type=F32)
            acc_scr[h] = alpha * acc_scr[h] + pv
            m_scr[h] = m_new[h]

    for qi in range(seq // tq):
        qs = []
        for h in heads:
            qt = qt_scr[hcols(h), qi * tq:(qi + 1) * tq]
            qs.append(jnp.concatenate(
                [jnp.concatenate([qt[:QK_HEAD_DIM], zeros], axis=0),
                 jnp.concatenate([zeros, qt[QK_HEAD_DIM:]], axis=0)], axis=1))
        m_scr[...] = jnp.full(m_scr.shape, NEG_INF, F32)
        acc_scr[...] = jnp.zeros(acc_scr.shape, F32)

        produce(qs, bufs[0], 0)
        npairs = qi // 2

        def pair(i, carry):
            base = pl.multiple_of(i * (2 * tq), 2 * tq)
            mid = pl.multiple_of(base + tq, tq)
            produce(qs, bufs[1], mid)
            consume(bufs[0], base, False)
            produce(qs, bufs[0], pl.multiple_of(base + 2 * tq, 2 * tq))
            consume(bufs[1], mid, False)
            return carry

        if npairs > 0:
            lax.fori_loop(0, npairs, pair, 0)
        last = 0
        if qi % 2:
            produce(qs, bufs[1], qi * tq)
            consume(bufs[0], (qi - 1) * tq, False)
            last = 1
        consume(bufs[last], qi * tq, True)

        for h in heads:
            acc = acc_scr[h]
            on = acc[0:V_HEAD_DIM] / acc[V_HEAD_DIM:V_HEAD_DIM + 1]
            d = on[:, :tq] - lam * on[:, tq:]
            ms = jnp.mean(d * d, axis=0, keepdims=True)
            y = d * lax.rsqrt(ms + EPS) * sub_ref[...] * (1.0 - lam_init)
            o_ref[qi * tq:(qi + 1) * tq, hcols(h)] = y.T.astype(o_ref.dtype)


def _attn_prompt(lams, sub, q, kb, vb, lam_init, batch, seq, tq=256):
    hp = HEADS_PER_STEP
    assert seq % tq == 0 and N_HEADS % hp == 0
    vec = pl.BlockSpec((1, QK_HEAD_DIM), lambda b, h: (0, 0))
    blk = pl.BlockSpec((seq, hp * V_HEAD_DIM), lambda b, h: (b, h))
    sub_cols = jnp.broadcast_to(sub.reshape(V_HEAD_DIM, 1), (V_HEAD_DIM, tq))
    return pl.pallas_call(
        functools.partial(_attn_prompt_kernel, lam_init=lam_init, seq=seq, tq=tq),
        grid=(batch, N_HEADS // hp),
        in_specs=[vec, vec, vec, vec, pl.BlockSpec((V_HEAD_DIM, tq), lambda b, h: (0, 0)),
                  blk, blk, blk],
        out_specs=blk,
        out_shape=jax.ShapeDtypeStruct((batch * seq, ATTN_WIDTH), BF16),
        scratch_shapes=[pltpu.VMEM((hp * V_HEAD_DIM, seq), BF16),
                        pltpu.VMEM((hp, V_HEAD_DIM + BF16_ROWS, seq), BF16),
                        pltpu.VMEM((hp, 1, 2 * tq), F32),
                        pltpu.VMEM((hp, V_HEAD_DIM + BF16_ROWS, 2 * tq), F32),
                        pltpu.VMEM((hp, tq, 2 * tq), F32),
                        pltpu.VMEM((hp, tq, 2 * tq), F32)],
        compiler_params=_cparams(("parallel", "parallel")),
        name="attn_prompt",
    )(*lams, sub_cols, q, kb, vb)


def _attn_sample_kernel(pt_ref, lq1, lk1, lq2, lk2, sub_ref, q_ref, kn_ref, vn_ref, *rest,
                        lam_init, n_new):
    g = PAGES_PER_STEP
    k_refs = rest[:g]
    v_refs = rest[g:2 * g]
    o_ref = rest[2 * g]
    m_scr, l_scr, acc_scr = rest[2 * g + 1:]
    step = pl.program_id(1)
    nrow = 2 * N_HEADS * n_new
    q = q_ref[...]
    nt_dims = (((1,), (1,)), ((), ()))

    def head_of_row(shape):
        r = lax.broadcasted_iota(jnp.int32, shape, 0)
        return (r % (N_HEADS * n_new)) // n_new, r % n_new

    def update(s, vmat):
        m = m_scr[...]
        m_new = jnp.maximum(m, jnp.max(s, axis=-1, keepdims=True))
        alpha = jnp.exp2(m - m_new)
        p = jnp.exp2(s - m_new)
        l_scr[...] = alpha * l_scr[...] + jnp.sum(p, axis=-1, keepdims=True)
        acc_scr[...] = alpha * acc_scr[...] + jnp.dot(p.astype(BF16), vmat,
                                                      preferred_element_type=F32)
        m_scr[...] = m_new

    @pl.when(step == 0)
    def _():
        m_scr[...] = jnp.full(m_scr.shape, NEG_INF, F32)
        l_scr[...] = jnp.zeros(l_scr.shape, F32)
        acc_scr[...] = jnp.zeros(acc_scr.shape, F32)
        kn = kn_ref[...].reshape(n_new * N_HEADS, V_HEAD_DIM).astype(BF16)
        vn = vn_ref[...].reshape(n_new * N_HEADS, V_HEAD_DIM).astype(BF16)
        s = lax.dot_general(q, kn, nt_dims, preferred_element_type=F32)
        shape = s.shape
        hrow, trow = head_of_row(shape)
        c = lax.broadcasted_iota(jnp.int32, shape, 1)
        ok = (hrow == c % N_HEADS) & (c // N_HEADS <= trow)
        update(jnp.where(ok, s, NEG_INF), vn)

    ncol = PAGE_SIZE * N_HEADS
    hrow, _ = head_of_row((nrow, ncol))
    ok = hrow == lax.broadcasted_iota(jnp.int32, (nrow, ncol), 1) % N_HEADS
    s_list = []
    m_new = m_scr[...]
    for p in range(g):
        kp = k_refs[p][...].reshape(ncol, V_HEAD_DIM).astype(BF16)
        s = lax.dot_general(q, kp, nt_dims, preferred_element_type=F32)
        s = jnp.where(ok, s, NEG_INF)
        s_list.append(s)
        m_new = jnp.maximum(m_new, jnp.max(s, axis=-1, keepdims=True))
    alpha = jnp.exp2(m_scr[...] - m_new)
    l = alpha * l_scr[...]
    acc = alpha * acc_scr[...]
    for p in range(g):
        pr = jnp.exp2(s_list[p] - m_new)
        l = l + jnp.sum(pr, axis=-1, keepdims=True)
        vp = v_refs[p][...].reshape(ncol, V_HEAD_DIM).astype(BF16)
        acc = acc + jnp.dot(pr.astype(BF16), vp, preferred_element_type=F32)
    m_scr[...] = m_new
    l_scr[...] = l
    acc_scr[...] = acc

    @pl.when(step == pl.num_programs(1) - 1)
    def _():
        lam = _lambda_full(lq1, lk1, lq2, lk2, lam_init)
        on = acc_scr[...] / l_scr[...]
        half = N_HEADS * n_new
        o = on[:half] - lam * on[half:]
        ms = jnp.mean(o * o, axis=-1, keepdims=True)
        o = o * lax.rsqrt(ms + EPS) * sub_ref[...] * (1.0 - lam_init)
        o_ref[...] = o.astype(o_ref.dtype)


def _attn_sample(page_table, lams, sub, q_rows, k_new, v_new, cache_k, cache_v, layer, lam_init):
    batch, n_pages = page_table.shape
    n_new = k_new.shape[1]
    g = PAGES_PER_STEP
    nrow = 2 * N_HEADS * n_new
    assert n_pages % g == 0
    vec = pl.BlockSpec((1, QK_HEAD_DIM), lambda b, s, pt: (0, 0))
    new_spec = pl.BlockSpec((None, n_new, N_HEADS, V_HEAD_DIM), lambda b, s, pt: (b, 0, 0, 0))

    def page_spec(r):
        return pl.BlockSpec((None, None, PAGE_SIZE, N_HEADS, V_HEAD_DIM),
                            lambda b, s, pt: (layer, pt[b, s * g + r], 0, 0, 0))

    in_specs = ([vec, vec, vec, vec, pl.BlockSpec((1, V_HEAD_DIM), lambda b, s, pt: (0, 0)),
                 pl.BlockSpec((None, nrow, V_HEAD_DIM), lambda b, s, pt: (b, 0, 0)),
                 new_spec, new_spec]
                + [page_spec(r) for r in range(g)] + [page_spec(r) for r in range(g)])
    grid_spec = pltpu.PrefetchScalarGridSpec(
        num_scalar_prefetch=1,
        grid=(batch, n_pages // g),
        in_specs=in_specs,
        out_specs=pl.BlockSpec((None, N_HEADS * n_new, V_HEAD_DIM), lambda b, s, pt: (b, 0, 0)),
        scratch_shapes=[pltpu.VMEM((nrow, 1), F32), pltpu.VMEM((nrow, 1), F32),
                        pltpu.VMEM((nrow, V_HEAD_DIM), F32)],
    )
    return pl.pallas_call(
        functools.partial(_attn_sample_kernel, lam_init=lam_init, n_new=n_new),
        grid_spec=grid_spec,
        out_shape=jax.ShapeDtypeStruct((batch, N_HEADS * n_new, V_HEAD_DIM), BF16),
        compiler_params=_cparams(("parallel", "arbitrary")),
        name="attn_sample",
    )(page_table, *lams, sub, q_rows, k_new, v_new, *([cache_k] * g), *([cache_v] * g))


CONV_PAD = 32
CONV_CHUNK = 32
CONV_UNROLL = 4
SUBLANES = 8


def _shift_matrix(nwin):
    span = nwin - SUBLANES
    r = jnp.arange((SUBLANES - 1) * span)
    src = r % span + r // span + 1
    one = (src[:, None] == jnp.arange(nwin)[None, :]).astype(BF16)
    return jnp.concatenate([one, one, one], axis=1)


def _conv_kernel(u_ref, buf_ref, w_ref, b_ref, lg_ref, lb_ref, shift_ref, o_ref, nb_ref, pad_scr,
                 *, t_pad, t_real):
    pad_scr[0:CONV_PAD, :] = buf_ref[...]
    pad_scr[CONV_PAD:CONV_PAD + t_pad, :] = u_ref[...]
    off = CONV_PAD - (CONV_KERNEL - 1)
    rc = min(CONV_CHUNK, t_pad)
    nwin = rc + CONV_PAD
    span = nwin - SUBLANES

    def chunk(c0):
        acc = jnp.broadcast_to(b_ref[...], (rc, CONV_WIDTH))
        win = pad_scr[pl.ds(c0, nwin), :]
        hi = win.astype(BF16)
        r1 = win - hi.astype(F32)
        mid = r1.astype(BF16)
        lo = (r1 - mid.astype(F32)).astype(BF16)
        shifted = jnp.dot(shift_ref[...], jnp.concatenate([hi, mid, lo], axis=0),
                          preferred_element_type=F32)
        for s in range(SUBLANES):
            taps = [j for j in range(CONV_KERNEL) if (off + j) % SUBLANES == s]
            sh = win if s == 0 else shifted[(s - 1) * span:s * span]
            for j in taps:
                a0 = off + j - s
                acc = acc + w_ref[j:j + 1, :] * sh[a0:a0 + rc, :]
        mu = jnp.mean(acc, axis=-1, keepdims=True)
        d = acc - mu
        var = jnp.mean(d * d, axis=-1, keepdims=True)
        y = d * lax.rsqrt(var + EPS) * lg_ref[...] + lb_ref[...]
        o_ref[pl.ds(c0, rc), :] = _silu(y).astype(o_ref.dtype)

    if t_pad == rc:
        chunk(0)
    else:
        def body(ci, carry):
            chunk(pl.multiple_of(ci * rc, rc))
            return carry
        lax.fori_loop(0, t_pad // rc, body, 0, unroll=CONV_UNROLL)
    nb_ref[...] = pad_scr[t_real:t_real + CONV_PAD, :]


def _conv_module(u, buf32, w, b, lg, lb, t_real):
    batch, t_pad, _ = u.shape
    vec = pl.BlockSpec((1, CONV_WIDTH), lambda i: (0, 0))
    shift = _shift_matrix(min(CONV_CHUNK, t_pad) + CONV_PAD)
    return pl.pallas_call(
        functools.partial(_conv_kernel, t_pad=t_pad, t_real=t_real),
        grid=(batch,),
        in_specs=[pl.BlockSpec((None, t_pad, CONV_WIDTH), lambda i: (i, 0, 0)),
                  pl.BlockSpec((None, CONV_PAD, CONV_WIDTH), lambda i: (i, 0, 0)),
                  pl.BlockSpec((CONV_PAD, CONV_WIDTH), lambda i: (0, 0)),
                  vec, vec, vec, pl.BlockSpec(shift.shape, lambda i: (0, 0))],
        out_specs=[pl.BlockSpec((None, t_pad, CONV_WIDTH), lambda i: (i, 0, 0)),
                   pl.BlockSpec((None, CONV_PAD, CONV_WIDTH), lambda i: (i, 0, 0))],
        out_shape=[jax.ShapeDtypeStruct((batch, t_pad, CONV_WIDTH), BF16),
                   jax.ShapeDtypeStruct((batch, CONV_PAD, CONV_WIDTH), F32)],
        scratch_shapes=[pltpu.VMEM((CONV_PAD + t_pad, CONV_WIDTH), F32)],
        compiler_params=_cparams(("parallel",)),
        name="conv_module",
    )(u, buf32, w, b, lg, lb, shift)


LRU_PAD = 8
LRU_CHUNK = 256


def _lru_kernel(gate_ref, xb_ref, buf_ref, h0_ref, cw_ref, cb_ref, wa_ref, ba_ref, wx_ref, bx_ref,
                lam_ref, y_ref, nb_ref, hl_ref, pad_scr, a_scr, b_scr, h_scr, *, t_pad, t_real):
    pad_scr[0:LRU_PAD, :] = buf_ref[...]
    pad_scr[LRU_PAD:LRU_PAD + t_pad, :] = xb_ref[...]
    off = LRU_PAD - (LRU_CONV - 1)
    rc = min(LRU_CHUNK, t_pad)
    lam = lam_ref[...]
    sp = jnp.maximum(-lam, 0.0) + _log1p(jnp.exp(-jnp.abs(lam)))

    def gates(c0):
        xc = jnp.broadcast_to(cb_ref[...], (rc, LRU_WIDTH))
        win = pad_scr[pl.ds(c0, rc + LRU_PAD), :]
        for j in range(LRU_CONV):
            xc = xc + cw_ref[j:j + 1, :] * win[off + j:off + j + rc, :]
        xcb = xc.astype(BF16)
        r = _sigmoid(jnp.dot(xcb, wa_ref[...], preferred_element_type=F32) + ba_ref[...])
        i = _sigmoid(jnp.dot(xcb, wx_ref[...], preferred_element_type=F32) + bx_ref[...])
        log_a = (-LRU_C) * r * sp
        a_scr[pl.ds(c0, rc), :] = jnp.exp(log_a)
        b_scr[pl.ds(c0, rc), :] = jnp.sqrt(-_expm1_nonpos(2.0 * log_a)) * (i * xc)

    if t_pad == rc:
        gates(0)
    else:
        def gbody(ci, carry):
            gates(pl.multiple_of(ci * rc, rc))
            return carry
        lax.fori_loop(0, t_pad // rc, gbody, 0)

    row = lax.broadcasted_iota(jnp.int32, (8, LRU_WIDTH), 0)

    def sbody(gi, h):
        r0 = pl.multiple_of(gi * 8, 8)
        a = a_scr[pl.ds(r0, 8), :]
        b = b_scr[pl.ds(r0, 8), :]
        for s in (1, 2, 4):
            a_sh = jnp.where(row >= s, pltpu.roll(a, s, 0), 1.0)
            b_sh = jnp.where(row >= s, pltpu.roll(b, s, 0), 0.0)
            b = a * b_sh + b
            a = a * a_sh
        hs = a * h + b
        h_scr[pl.ds(r0, 8), :] = hs
        return hs[7:8, :]

    lax.fori_loop(0, t_pad // 8, sbody, h0_ref[...])
    y_ref[...] = (h_scr[...] * gate_ref[...]).astype(y_ref.dtype)
    nb_ref[...] = pad_scr[t_real:t_real + LRU_PAD, :]
    hl_ref[...] = h_scr[t_real - 1:t_real, :]


def _lru_block(gate, xb, buf8, h0, cw, cb, wa, ba, wx, bx, lam, layer, t_real):
    batch, t_pad, _ = gate.shape
    vec = pl.BlockSpec((1, LRU_WIDTH), lambda i: (0, 0))
    seq = pl.BlockSpec((None, t_pad, LRU_WIDTH), lambda i: (i, 0, 0))
    mat = pl.BlockSpec((None, LRU_WIDTH, LRU_WIDTH), lambda i: (layer, 0, 0))
    return pl.pallas_call(
        functools.partial(_lru_kernel, t_pad=t_pad, t_real=t_real),
        grid=(batch,),
        in_specs=[seq, seq,
                  pl.BlockSpec((None, LRU_PAD, LRU_WIDTH), lambda i: (i, 0, 0)),
                  pl.BlockSpec((None, 1, LRU_WIDTH), lambda i: (i, 0, 0)),
                  pl.BlockSpec((LRU_CONV, LRU_WIDTH), lambda i: (0, 0)),
                  vec, mat, vec, mat, vec, vec],
        out_specs=[seq,
                   pl.BlockSpec((None, LRU_PAD, LRU_WIDTH), lambda i: (i, 0, 0)),
                   pl.BlockSpec((None, 1, LRU_WIDTH), lambda i: (i, 0, 0))],
        out_shape=[jax.ShapeDtypeStruct((batch, t_pad, LRU_WIDTH), BF16),
                   jax.ShapeDtypeStruct((batch, LRU_PAD, LRU_WIDTH), F32),
                   jax.ShapeDtypeStruct((batch, 1, LRU_WIDTH), F32)],
        scratch_shapes=[pltpu.VMEM((LRU_PAD + t_pad, LRU_WIDTH), F32),
                        pltpu.VMEM((t_pad, LRU_WIDTH), F32),
                        pltpu.VMEM((t_pad, LRU_WIDTH), F32),
                        pltpu.VMEM((t_pad, LRU_WIDTH), F32)],
        compiler_params=_cparams(("parallel",)),
        name="lru_block",
    )(gate, xb, buf8, h0, cw, cb, wa, ba, wx, bx, lam)


def _out_proj_kernel(x_ref, oa_ref, oc_ref, ol_ref, w_ref, y_ref):
    acc = x_ref[...]
    acc = acc + jnp.dot(oa_ref[...], w_ref[0:ATTN_WIDTH, :], preferred_element_type=F32)
    acc = acc + jnp.dot(oc_ref[...], w_ref[ATTN_WIDTH:ATTN_WIDTH + CONV_WIDTH, :],
                        preferred_element_type=F32)
    acc = acc + jnp.dot(ol_ref[...], w_ref[ATTN_WIDTH + CONV_WIDTH:, :],
                        preferred_element_type=F32)
    y_ref[...] = acc


def _out_proj(x, oa, oc, ol, w, layer):
    m = x.shape[0]
    tm, _ = _tiles(m)
    assert m % tm == 0
    row = lambda i: (i, 0)
    return pl.pallas_call(
        _out_proj_kernel,
        grid=(m // tm,),
        in_specs=[pl.BlockSpec((tm, D_MODEL), row), pl.BlockSpec((tm, ATTN_WIDTH), row),
                  pl.BlockSpec((tm, CONV_WIDTH), row), pl.BlockSpec((tm, LRU_WIDTH), row),
                  pl.BlockSpec((None, D_MODEL, D_MODEL), lambda i: (layer, 0, 0))],
        out_specs=pl.BlockSpec((tm, D_MODEL), row),
        out_shape=jax.ShapeDtypeStruct((m, D_MODEL), F32),
        compiler_params=_cparams(("parallel",)),
        name="out_proj",
    )(x, oa, oc, ol, w)


def _ffn_kernel(x_ref, g_ref, wg_ref, wu_ref, wd_ref, y_ref, xn_scr):
    f = pl.program_id(1)

    @pl.when(f == 0)
    def _():
        x = x_ref[...]
        ms = jnp.mean(x * x, axis=-1, keepdims=True)
        xn_scr[...] = (x * lax.rsqrt(ms + EPS) * g_ref[...]).astype(BF16)
        y_ref[...] = x

    xn = xn_scr[...]
    gate = jnp.dot(xn, wg_ref[...], preferred_element_type=F32)
    up = jnp.dot(xn, wu_ref[...], preferred_element_type=F32)
    hid = (_silu(gate) * up).astype(BF16)
    y_ref[...] += jnp.dot(hid, wd_ref[...], preferred_element_type=F32)


def _ffn(x, g, wg, wu, wd, layer):
    m = x.shape[0]
    tm, tf = _tiles(m)
    d_ff = wg.shape[-1]
    assert m % tm == 0 and d_ff % tf == 0
    row = lambda i, f: (i, 0)
    return pl.pallas_call(
        _ffn_kernel,
        grid=(m // tm, d_ff // tf),
        in_specs=[pl.BlockSpec((tm, D_MODEL), row),
                  pl.BlockSpec((1, D_MODEL), lambda i, f: (0, 0)),
                  pl.BlockSpec((None, D_MODEL, tf), lambda i, f: (layer, 0, f)),
                  pl.BlockSpec((None, D_MODEL, tf), lambda i, f: (layer, 0, f)),
                  pl.BlockSpec((None, tf, D_MODEL), lambda i, f: (layer, f, 0))],
        out_specs=pl.BlockSpec((tm, D_MODEL), row),
        out_shape=jax.ShapeDtypeStruct((m, D_MODEL), F32),
        scratch_shapes=[pltpu.VMEM((tm, D_MODEL), BF16)],
        compiler_params=_cparams(("parallel", "arbitrary")),
        name="ffn",
    )(x, g, wg, wu, wd)


def _rope_tables(pos):
    half = ROPE_DIM // 2
    inv_freq = ROPE_THETA ** (-jnp.arange(0, ROPE_DIM, 2, dtype=F32) / ROPE_DIM)
    ang = pos.astype(F32)[:, None] * inv_freq[None, :]
    cos, sin = jnp.cos(ang), jnp.sin(ang)
    t = pos.shape[0]
    ones = jnp.ones((t, QK_HEAD_DIM - ROPE_DIM), F32)
    zeros = jnp.zeros((t, QK_HEAD_DIM - ROPE_DIM), F32)
    zh = jnp.zeros((t, half), F32)
    rc = jnp.concatenate([cos, cos, ones], axis=1)
    rs1 = jnp.concatenate([-sin, zh, zeros], axis=1)
    rs2 = jnp.concatenate([zh, sin, zeros], axis=1)
    return tuple(jnp.tile(a, (1, MXU_COLS // QK_HEAD_DIM)) for a in (rc, rs1, rs2))


def _block_diag(w):
    h, di, dj = w.shape
    eye = jnp.eye(h, dtype=w.dtype)
    return (eye[:, None, :, None] * w[:, :, None, :]).reshape(h * di, h * dj)


def _group_mean_matrix():
    r = jnp.arange(MXU_COLS) // QK_HEAD_DIM
    return ((r[:, None] == r[None, :]).astype(F32) / QK_HEAD_DIM).astype(BF16)


def _layer(x, p, l, rope, batch, t_real, past):
    lam_init = 0.8 - 0.6 * math.exp(-0.3 * l)
    m = x.shape[0]
    row = lambda a: a[l].reshape(1, -1)
    gain = lambda a: jnp.tile(a[l], MXU_COLS // QK_HEAD_DIM).reshape(1, MXU_COLS)
    lams = (row(p['lambda_q1']), row(p['lambda_k1']), row(p['lambda_q2']), row(p['lambda_k2']))
    sub = row(p['subln'])

    q, k, kb, v, vb, u, gate, xb = _in_proj(
        x, row(p['norm_mix']), p['w_in_bf'], l, gain(p['q_norm']), gain(p['k_norm']),
        p['gmat'], *rope)

    if past is None:
        o_attn = _attn_prompt(lams, sub, q, kb, vb, lam_init, batch, t_real)
        conv_buf = jnp.zeros((batch, CONV_PAD, CONV_WIDTH), F32)
        lru_buf = jnp.zeros((batch, LRU_PAD, LRU_WIDTH), F32)
        h0 = jnp.zeros((batch, 1, LRU_WIDTH), F32)
    else:
        cache_k, cache_v, state_conv, state_lru_conv, state_lru_h, page_table = past
        q5 = q.reshape(batch, t_real, N_HEADS, 2, QK_HEAD_DIM)
        zero = jnp.zeros_like(q5[:, :, :, 0])
        q_rows = jnp.stack([jnp.concatenate([q5[:, :, :, 0], zero], axis=-1),
                            jnp.concatenate([zero, q5[:, :, :, 1]], axis=-1)], axis=1)
        q_rows = q_rows.transpose(0, 1, 3, 2, 4).reshape(batch, 2 * N_HEADS * t_real, V_HEAD_DIM)
        o = _attn_sample(page_table, lams, sub, q_rows,
                         k.reshape(batch, t_real, N_HEADS, V_HEAD_DIM),
                         v.reshape(batch, t_real, N_HEADS, V_HEAD_DIM),
                         cache_k, cache_v, l, lam_init)
        o_attn = (o.reshape(batch, N_HEADS, t_real, V_HEAD_DIM).transpose(0, 2, 1, 3)
                  .reshape(m, ATTN_WIDTH))
        conv_buf = jnp.pad(state_conv[l], ((0, 0), (CONV_PAD - (CONV_KERNEL - 1), 0), (0, 0)))
        lru_buf = jnp.pad(state_lru_conv[l], ((0, 0), (LRU_PAD - (LRU_CONV - 1), 0), (0, 0)))
        h0 = state_lru_h[l].reshape(batch, 1, LRU_WIDTH)

    def seq3(a, rows):
        a = a.reshape(batch, t_real, a.shape[-1])
        pad = -t_real % rows
        return a if pad == 0 else jnp.pad(a, ((0, 0), (0, pad), (0, 0)))

    conv_w = jnp.pad(p['conv_w'][l], ((0, CONV_PAD - CONV_KERNEL), (0, 0)))
    o_conv, nb_conv = _conv_module(seq3(u, BF16_ROWS), conv_buf, conv_w, row(p['conv_b']),
                                   row(p['conv_ln_g']), row(p['conv_ln_b']), t_real)
    o_lru, nb_lru, h_last = _lru_block(
        seq3(gate, SUBLANES), seq3(xb, SUBLANES), lru_buf, h0, p['lru_conv_w'][l],
        row(p['lru_conv_b']),
        p['lru_wa_bd'], row(p['lru_ba']), p['lru_wx_bd'], row(p['lru_bx']),
        row(p['lru_lambda']), l, t_real)
    o_conv = o_conv[:, :t_real].reshape(m, CONV_WIDTH)
    o_lru = o_lru[:, :t_real].reshape(m, LRU_WIDTH)

    x = _out_proj(x, o_attn, o_conv, o_lru, p['w_out_bf'], l)
    x = _ffn(x, row(p['norm_ffn']), p['w_gate_bf'], p['w_up_bf'], p['w_down_bf'], l)
    states = (k.reshape(batch, t_real, N_HEADS, 2 * QK_HEAD_DIM),
              v.reshape(batch, t_real, N_HEADS, V_HEAD_DIM),
              nb_conv[:, CONV_PAD - (CONV_KERNEL - 1):],
              nb_lru[:, LRU_PAD - (LRU_CONV - 1):],
              h_last.reshape(batch, LRU_WIDTH))
    return x, states


def _trunk(x, start_pos, p, past):
    batch, t, _ = x.shape
    depth = p['w_in_bf'].shape[0]
    tm = _in_proj_tile(batch * t)
    pos = start_pos + jnp.arange(t, dtype=jnp.int32)
    rope = _rope_tables(pos)
    if t < tm:
        rope = tuple(jnp.tile(a, (tm // t, 1)) for a in rope)
    x = x.reshape(batch * t, D_MODEL)
    outs = []
    for l in range(depth):
        x, st = _layer(x, p, l, rope, batch, t, past)
        outs.append(st)
    stacked = tuple(jnp.stack([o[i] for o in outs]) for i in range(5))
    return (x.reshape(batch, t, D_MODEL),) + stacked


def kernel(x_prompt, x_sample, cache_k, cache_v, state_conv, state_lru_conv, state_lru_h, page_table,
           norm_mix, w_in, q_norm, k_norm, lambda_q1, lambda_k1, lambda_q2, lambda_k2, subln,
           conv_w, conv_b, conv_ln_g, conv_ln_b, lru_conv_w, lru_conv_b, lru_wa, lru_ba, lru_wx, lru_bx,
           lru_lambda, w_out, norm_ffn, w_ffn_gate, w_ffn_up, w_ffn_down):
    p = dict(norm_mix=norm_mix, q_norm=q_norm, k_norm=k_norm,
             lambda_q1=lambda_q1, lambda_k1=lambda_k1, lambda_q2=lambda_q2, lambda_k2=lambda_k2,
             subln=subln, conv_w=conv_w, conv_b=conv_b, conv_ln_g=conv_ln_g, conv_ln_b=conv_ln_b,
             lru_conv_w=lru_conv_w, lru_conv_b=lru_conv_b, lru_ba=lru_ba, lru_bx=lru_bx,
             lru_lambda=lru_lambda, norm_ffn=norm_ffn)
    p['w_in_bf'] = w_in.astype(BF16)
    p['w_out_bf'] = w_out.astype(BF16)
    p['w_gate_bf'] = w_ffn_gate.astype(BF16)
    p['w_up_bf'] = w_ffn_up.astype(BF16)
    p['w_down_bf'] = w_ffn_down.astype(BF16)
    p['lru_wa_bd'] = jax.vmap(_block_diag)(lru_wa).astype(BF16)
    p['lru_wx_bd'] = jax.vmap(_block_diag)(lru_wx).astype(BF16)
    p['gmat'] = _group_mean_matrix()

    y_p, k_p, v_p, cb_p, lcb_p, h_p = _trunk(x_prompt, 0, p, None)
    past_len = page_table.shape[1] * PAGE_SIZE
    y_s, k_s, v_s, cb_s, lcb_s, h_s = _trunk(
        x_sample, past_len, p,
        (cache_k, cache_v, state_conv, state_lru_conv, state_lru_h, page_table))
    return (y_p, y_s, k_p, v_p, cb_p, lcb_p, h_p, k_s, v_s, cb_s, lcb_s, h_s)
```

```python
import functools
import math

import jax
import jax.numpy as jnp
from jax import lax
from jax.experimental import pallas as pl
from jax.experimental.pallas import tpu as pltpu

F32 = jnp.float32
BF16 = jnp.bfloat16

D_MODEL = 2048
N_HEADS = 8
V_HEAD_DIM = 128
QK_HEAD_DIM = 64
ROPE_DIM = 16
ROPE_THETA = 500000.0
ATTN_SCALE = 1.0 / math.sqrt(QK_HEAD_DIM)
LOG2_E = 1.4426950408889634
Q_SCALE = ATTN_SCALE * LOG2_E
ATTN_WIDTH = N_HEADS * V_HEAD_DIM
CONV_WIDTH = 512
LRU_WIDTH = 512
CONV_KERNEL = 31
LRU_CONV = 4
LRU_C = 8.0
EPS = 1e-6
NEG_INF = -1e30
PAGE_SIZE = 128
PAGES_PER_STEP = 8
GROUP_COLS = 1024
MXU_COLS = 256
BF16_ROWS = 16
VMEM_LIMIT = 56 * 1024 * 1024


def _tiles(m):
    return min(512, m), 512


def _in_proj_tile(m):
    return min(256, m)


def _cparams(sem):
    return pltpu.CompilerParams(dimension_semantics=sem, vmem_limit_bytes=VMEM_LIMIT)


def _sigmoid(x):
    return 1.0 / (1.0 + jnp.exp(-x))


def _silu(x):
    return x * _sigmoid(x)


def _gelu_tanh(x):
    c = math.sqrt(2.0 / math.pi)
    return x * (0.5 * (1.0 + jnp.tanh(c * (x + 0.044715 * (x * x * x)))))


def _log1p(z):
    w = 1.0 + z
    small = w == 1.0
    return jnp.where(small, z, jnp.log(w) * z / jnp.where(small, 1.0, w - 1.0))


def _expm1_nonpos(x):
    u = jnp.exp(x)
    direct = (u == 1.0) | (x < -20.0)
    ratio = (u - 1.0) * x / jnp.where(direct, 1.0, jnp.log(u))
    return jnp.where(u == 1.0, x, jnp.where(x < -20.0, u - 1.0, ratio))


def _lambda_full(lq1, lk1, lq2, lk2, lam_init):
    s1 = jnp.sum(lq1[...] * lk1[...], axis=-1, keepdims=True)
    s2 = jnp.sum(lq2[...] * lk2[...], axis=-1, keepdims=True)
    return jnp.exp(s1) - jnp.exp(s2) + lam_init


def _qk_chunk(y, gain, gmat, rc, rs1, rs2, scale):
    ms = jnp.dot((y * y).astype(BF16), gmat, preferred_element_type=F32)
    yn = y * lax.rsqrt(ms + EPS) * gain
    rot = yn * rc + pltpu.roll(yn, MXU_COLS - ROPE_DIM // 2, 1) * rs1 \
        + pltpu.roll(yn, ROPE_DIM // 2, 1) * rs2
    return rot * scale if scale != 1.0 else rot


def _in_proj_kernel(x_ref, g_ref, w_ref, qn_ref, kn_ref, gmat_ref, rc_ref, rs1_ref, rs2_ref,
                    q_ref, k_ref, kb_ref, v_ref, vb_ref, u_ref, gate_ref, xb_ref, xn_scr):
    x = x_ref[...]
    ms = jnp.mean(x * x, axis=-1, keepdims=True)
    xn_scr[...] = (x * lax.rsqrt(ms + EPS) * g_ref[...]).astype(BF16)

    def cols(c):
        return slice(c * MXU_COLS, (c + 1) * MXU_COLS)

    def proj(group):
        w = w_ref[:, group * GROUP_COLS:(group + 1) * GROUP_COLS]
        return jnp.dot(xn_scr[...], w, preferred_element_type=F32)

    y = proj(0)
    for c in range(GROUP_COLS // MXU_COLS):
        r = _qk_chunk(y[:, cols(c)], qn_ref[...], gmat_ref[...], rc_ref[...], rs1_ref[...],
                      rs2_ref[...], Q_SCALE)
        q_ref[:, cols(c)] = r.astype(BF16)

    y = proj(1)
    for c in range(GROUP_COLS // MXU_COLS):
        r = _qk_chunk(y[:, cols(c)], kn_ref[...], gmat_ref[...], rc_ref[...], rs1_ref[...],
                      rs2_ref[...], 1.0)
        k_ref[:, cols(c)] = r
        kb_ref[:, cols(c)] = r.astype(BF16)

    y = proj(2)
    v_ref[...] = y
    vb_ref[...] = y.astype(BF16)

    y = proj(3)
    u_ref[...] = y[:, :CONV_WIDTH] * _sigmoid(y[:, CONV_WIDTH:])

    y = proj(4)
    gate_ref[...] = _gelu_tanh(y[:, :LRU_WIDTH])
    xb_ref[...] = y[:, LRU_WIDTH:]


def _in_proj(x, g, w, layer, qn, kn, gmat, rc, rs1, rs2):
    m = x.shape[0]
    tm = _in_proj_tile(m)
    assert m % tm == 0 and rc.shape[0] % tm == 0
    nt = m // tm
    nrope = rc.shape[0] // tm
    row = lambda i: (i, 0)
    const = lambda i: (0, 0)
    rope = lambda i: (i % nrope, 0)
    out_shape = [
        jax.ShapeDtypeStruct((m, ATTN_WIDTH), BF16),
        jax.ShapeDtypeStruct((m, ATTN_WIDTH), F32),
        jax.ShapeDtypeStruct((m, ATTN_WIDTH), BF16),
        jax.ShapeDtypeStruct((m, ATTN_WIDTH), F32),
        jax.ShapeDtypeStruct((m, ATTN_WIDTH), BF16),
        jax.ShapeDtypeStruct((m, CONV_WIDTH), F32),
        jax.ShapeDtypeStruct((m, LRU_WIDTH), F32),
        jax.ShapeDtypeStruct((m, LRU_WIDTH), F32),
    ]
    out_specs = [
        pl.BlockSpec((tm, ATTN_WIDTH), row), pl.BlockSpec((tm, ATTN_WIDTH), row),
        pl.BlockSpec((tm, ATTN_WIDTH), row), pl.BlockSpec((tm, ATTN_WIDTH), row),
        pl.BlockSpec((tm, ATTN_WIDTH), row), pl.BlockSpec((tm, CONV_WIDTH), row),
        pl.BlockSpec((tm, LRU_WIDTH), row), pl.BlockSpec((tm, LRU_WIDTH), row),
    ]
    in_specs = [
        pl.BlockSpec((tm, D_MODEL), row),
        pl.BlockSpec((1, D_MODEL), const),
        pl.BlockSpec((None, D_MODEL, 5 * GROUP_COLS), lambda i: (layer, 0, 0),
                     pipeline_mode=pl.Buffered(1)),
        pl.BlockSpec((1, MXU_COLS), const), pl.BlockSpec((1, MXU_COLS), const),
        pl.BlockSpec((MXU_COLS, MXU_COLS), const),
        pl.BlockSpec((tm, MXU_COLS), rope), pl.BlockSpec((tm, MXU_COLS), rope),
        pl.BlockSpec((tm, MXU_COLS), rope),
    ]
    return pl.pallas_call(
        _in_proj_kernel,
        grid=(nt,),
        in_specs=in_specs,
        out_specs=out_specs,
        out_shape=out_shape,
        scratch_shapes=[pltpu.VMEM((tm, D_MODEL), BF16)],
        compiler_params=_cparams(("parallel",)),
        name="in_proj",
    )(x, g, w, qn, kn, gmat, rc, rs1, rs2)


HEADS_PER_STEP = 2


def _attn_prompt_kernel(lq1, lk1, lq2, lk2, sub_ref, q_ref, k_ref, v_ref, o_ref,
                        qt_scr, vt_scr, m_scr, acc_scr, sa_scr, sb_scr, *, lam_init, seq, tq):
    lam = _lambda_full(lq1, lk1, lq2, lk2, lam_init)
    heads = range(HEADS_PER_STEP)
    hcols = lambda h: slice(h * V_HEAD_DIM, (h + 1) * V_HEAD_DIM)
    for c in range(seq // tq):
        sl = slice(c * tq, (c + 1) * tq)
        qt_scr[:, sl] = q_ref[sl, :].astype(F32).T.astype(BF16)
        vt = v_ref[sl, :].astype(F32).T.astype(BF16)
        for h in heads:
            vt_scr[h, 0:V_HEAD_DIM, sl] = vt[hcols(h)]
    first_row = lax.broadcasted_iota(jnp.int32, (BF16_ROWS, seq), 0) == 0
    for h in heads:
        vt_scr[h, V_HEAD_DIM:V_HEAD_DIM + BF16_ROWS, :] = \
            jnp.where(first_row, 1.0, 0.0).astype(BF16)

    key = lax.broadcasted_iota(jnp.int32, (tq, 2 * tq), 0)
    col = lax.broadcasted_iota(jnp.int32, (tq, 2 * tq), 1)
    causal = key <= jnp.where(col >= tq, col - tq, col)
    zeros = jnp.zeros((QK_HEAD_DIM, tq), BF16)

    bufs = (sa_scr, sb_scr)

    def produce(qs, buf, start):
        for h in heads:
            buf[h] = jnp.dot(k_ref[pl.ds(start, tq), hcols(h)], qs[h],
                             preferred_element_type=F32)

    def consume(buf, start, masked):
        s = [buf[h] for h in heads]
        if masked:
            s = [jnp.where(causal, sh, NEG_INF) for sh in s]
        m_new = [jnp.maximum(m_scr[h], jnp.max(s[h], axis=0, keepdims=True)) for h in heads]
        for h in heads:
            alpha = jnp.exp2(m_scr[h] - m_new[h])
            p = jnp.exp2(s[h] - m_new[h]).astype(BF16)
            pv = jnp.dot(vt_scr[h, :, pl.ds(start, tq)], p, preferred_element_type=F32)
            acc_scr[h] = alpha * acc_scr[h] + pv
            m_scr[h] = m_new[h]

    for qi in range(seq // tq):
        qs = []
        for h in heads:
            qt = qt_scr[hcols(h), qi * tq:(qi + 1) * tq]
            qs.append(jnp.concatenate(
                [jnp.concatenate([qt[:QK_HEAD_DIM], zeros], axis=0),
                 jnp.concatenate([zeros, qt[QK_HEAD_DIM:]], axis=0)], axis=1))
        m_scr[...] = jnp.full(m_scr.shape, NEG_INF, F32)
        acc_scr[...] = jnp.zeros(acc_scr.shape, F32)

        produce(qs, bufs[0], 0)
        npairs = qi // 2

        def pair(i, carry):
            base = pl.multiple_of(i * (2 * tq), 2 * tq)
            mid = pl.multiple_of(base + tq, tq)
            produce(qs, bufs[1], mid)
            consume(bufs[0], base, False)
            produce(qs, bufs[0], pl.multiple_of(base + 2 * tq, 2 * tq))
            consume(bufs[1], mid, False)
            return carry

        if npairs > 0:
            lax.fori_loop(0, npairs, pair, 0)
        last = 0
        if qi % 2:
            produce(qs, bufs[1], qi * tq)
            consume(bufs[0], (qi - 1) * tq, False)
            last = 1
        consume(bufs[last], qi * tq, True)

        for h in heads:
            acc = acc_scr[h]
            on = acc[0:V_HEAD_DIM] / acc[V_HEAD_DIM:V_HEAD_DIM + 1]
            d = on[:, :tq] - lam * on[:, tq:]
            ms = jnp.mean(d * d, axis=0, keepdims=True)
            y = d * lax.rsqrt(ms + EPS) * sub_ref[...] * (1.0 - lam_init)
            o_ref[qi * tq:(qi + 1) * tq, hcols(h)] = y.T.astype(o_ref.dtype)


def _attn_prompt(lams, sub, q, kb, vb, lam_init, batch, seq, tq=256):
    hp = HEADS_PER_STEP
    assert seq % tq == 0 and N_HEADS % hp == 0
    vec = pl.BlockSpec((1, QK_HEAD_DIM), lambda b, h: (0, 0))
    blk = pl.BlockSpec((seq, hp * V_HEAD_DIM), lambda b, h: (b, h))
    sub_cols = jnp.broadcast_to(sub.reshape(V_HEAD_DIM, 1), (V_HEAD_DIM, tq))
    return pl.pallas_call(
        functools.partial(_attn_prompt_kernel, lam_init=lam_init, seq=seq, tq=tq),
        grid=(batch, N_HEADS // hp),
        in_specs=[vec, vec, vec, vec, pl.BlockSpec((V_HEAD_DIM, tq), lambda b, h: (0, 0)),
                  blk, blk, blk],
        out_specs=blk,
        out_shape=jax.ShapeDtypeStruct((batch * seq, ATTN_WIDTH), BF16),
        scratch_shapes=[pltpu.VMEM((hp * V_HEAD_DIM, seq), BF16),
                        pltpu.VMEM((hp, V_HEAD_DIM + BF16_ROWS, seq), BF16),
                        pltpu.VMEM((hp, 1, 2 * tq), F32),
                        pltpu.VMEM((hp, V_HEAD_DIM + BF16_ROWS, 2 * tq), F32),
                        pltpu.VMEM((hp, tq, 2 * tq), F32),
                        pltpu.VMEM((hp, tq, 2 * tq), F32)],
        compiler_params=_cparams(("parallel", "parallel")),
        name="attn_prompt",
    )(*lams, sub_cols, q, kb, vb)


def _decode_step(step, nsteps, lams, sub_ref, q_ref, kn_ref, vn_ref, k_refs, v_refs, o_ref,
                 m_scr, l_scr, acc_scr, *, lam_init, n_new):
    g = PAGES_PER_STEP
    lq1, lk1, lq2, lk2 = lams
    nrow = 2 * N_HEADS * n_new
    q = q_ref[...]
    nt_dims = (((1,), (1,)), ((), ()))

    def head_of_row(shape):
        r = lax.broadcasted_iota(jnp.int32, shape, 0)
        return (r % (N_HEADS * n_new)) // n_new, r % n_new

    def update(s, vmat):
        m = m_scr[...]
        m_new = jnp.maximum(m, jnp.max(s, axis=-1, keepdims=True))
        alpha = jnp.exp2(m - m_new)
        p = jnp.exp2(s - m_new)
        l_scr[...] = alpha * l_scr[...] + jnp.sum(p, axis=-1, keepdims=True)
        acc_scr[...] = alpha * acc_scr[...] + jnp.dot(p.astype(BF16), vmat,
                                                      preferred_element_type=F32)
        m_scr[...] = m_new

    @pl.when(step == 0)
    def _():
        m_scr[...] = jnp.full(m_scr.shape, NEG_INF, F32)
        l_scr[...] = jnp.zeros(l_scr.shape, F32)
        acc_scr[...] = jnp.zeros(acc_scr.shape, F32)
        kn = kn_ref[...].reshape(n_new * N_HEADS, V_HEAD_DIM).astype(BF16)
        vn = vn_ref[...].reshape(n_new * N_HEADS, V_HEAD_DIM).astype(BF16)
        s = lax.dot_general(q, kn, nt_dims, preferred_element_type=F32)
        shape = s.shape
        hrow, trow = head_of_row(shape)
        c = lax.broadcasted_iota(jnp.int32, shape, 1)
        ok = (hrow == c % N_HEADS) & (c // N_HEADS <= trow)
        update(jnp.where(ok, s, NEG_INF), vn)

    ncol = PAGE_SIZE * N_HEADS
    hrow, _ = head_of_row((nrow, ncol))
    ok = hrow == lax.broadcasted_iota(jnp.int32, (nrow, ncol), 1) % N_HEADS
    s_list = []
    m_new = m_scr[...]
    for p in range(g):
        kp = k_refs[p][...].reshape(ncol, V_HEAD_DIM).astype(BF16)
        s = lax.dot_general(q, kp, nt_dims, preferred_element_type=F32)
        s = jnp.where(ok, s, NEG_INF)
        s_list.append(s)
        m_new = jnp.maximum(m_new, jnp.max(s, axis=-1, keepdims=True))
    alpha = jnp.exp2(m_scr[...] - m_new)
    l = alpha * l_scr[...]
    acc = alpha * acc_scr[...]
    for p in range(g):
        pr = jnp.exp2(s_list[p] - m_new)
        l = l + jnp.sum(pr, axis=-1, keepdims=True)
        vp = v_refs[p][...].reshape(ncol, V_HEAD_DIM).astype(BF16)
        acc = acc + jnp.dot(pr.astype(BF16), vp, preferred_element_type=F32)
    m_scr[...] = m_new
    l_scr[...] = l
    acc_scr[...] = acc

    @pl.when(step == nsteps - 1)
    def _():
        lam = _lambda_full(lq1, lk1, lq2, lk2, lam_init)
        on = acc_scr[...] / l_scr[...]
        half = N_HEADS * n_new
        o = on[:half] - lam * on[half:]
        ms = jnp.mean(o * o, axis=-1, keepdims=True)
        o = o * lax.rsqrt(ms + EPS) * sub_ref[...] * (1.0 - lam_init)
        o_ref[...] = o.astype(o_ref.dtype)


CONV_PAD = 32
CONV_CHUNK = 32
CONV_UNROLL = 4
SUBLANES = 8


def _shift_matrix(nwin):
    span = nwin - SUBLANES
    r = jnp.arange((SUBLANES - 1) * span)
    src = r % span + r // span + 1
    one = (src[:, None] == jnp.arange(nwin)[None, :]).astype(BF16)
    return jnp.concatenate([one, one, one], axis=1)


def _conv_kernel(u_ref, buf_ref, w_ref, b_ref, lg_ref, lb_ref, shift_ref, o_ref, nb_ref, pad_scr,
                 *, t_pad, t_real):
    pad_scr[0:CONV_PAD, :] = buf_ref[...]
    pad_scr[CONV_PAD:CONV_PAD + t_pad, :] = u_ref[...]
    off = CONV_PAD - (CONV_KERNEL - 1)
    rc = min(CONV_CHUNK, t_pad)
    nwin = rc + CONV_PAD
    span = nwin - SUBLANES

    def chunk(c0):
        acc = jnp.broadcast_to(b_ref[...], (rc, CONV_WIDTH))
        win = pad_scr[pl.ds(c0, nwin), :]
        hi = win.astype(BF16)
        r1 = win - hi.astype(F32)
        mid = r1.astype(BF16)
        lo = (r1 - mid.astype(F32)).astype(BF16)
        shifted = jnp.dot(shift_ref[...], jnp.concatenate([hi, mid, lo], axis=0),
                          preferred_element_type=F32)
        for s in range(SUBLANES):
            taps = [j for j in range(CONV_KERNEL) if (off + j) % SUBLANES == s]
            sh = win if s == 0 else shifted[(s - 1) * span:s * span]
            for j in taps:
                a0 = off + j - s
                acc = acc + w_ref[j:j + 1, :] * sh[a0:a0 + rc, :]
        mu = jnp.mean(acc, axis=-1, keepdims=True)
        d = acc - mu
        var = jnp.mean(d * d, axis=-1, keepdims=True)
        y = d * lax.rsqrt(var + EPS) * lg_ref[...] + lb_ref[...]
        o_ref[pl.ds(c0, rc), :] = _silu(y).astype(o_ref.dtype)

    if t_pad == rc:
        chunk(0)
    else:
        def body(ci, carry):
            chunk(pl.multiple_of(ci * rc, rc))
            return carry
        lax.fori_loop(0, t_pad // rc, body, 0, unroll=CONV_UNROLL)
    nb_ref[...] = pad_scr[t_real:t_real + CONV_PAD, :]


def _conv_module(u, buf32, w, b, lg, lb, t_real):
    batch, t_pad, _ = u.shape
    vec = pl.BlockSpec((1, CONV_WIDTH), lambda i: (0, 0))
    shift = _shift_matrix(min(CONV_CHUNK, t_pad) + CONV_PAD)
    return pl.pallas_call(
        functools.partial(_conv_kernel, t_pad=t_pad, t_real=t_real),
        grid=(batch,),
        in_specs=[pl.BlockSpec((None, t_pad, CONV_WIDTH), lambda i: (i, 0, 0)),
                  pl.BlockSpec((None, CONV_PAD, CONV_WIDTH), lambda i: (i, 0, 0)),
                  pl.BlockSpec((CONV_PAD, CONV_WIDTH), lambda i: (0, 0)),
                  vec, vec, vec, pl.BlockSpec(shift.shape, lambda i: (0, 0))],
        out_specs=[pl.BlockSpec((None, t_pad, CONV_WIDTH), lambda i: (i, 0, 0)),
                   pl.BlockSpec((None, CONV_PAD, CONV_WIDTH), lambda i: (i, 0, 0))],
        out_shape=[jax.ShapeDtypeStruct((batch, t_pad, CONV_WIDTH), BF16),
                   jax.ShapeDtypeStruct((batch, CONV_PAD, CONV_WIDTH), F32)],
        scratch_shapes=[pltpu.VMEM((CONV_PAD + t_pad, CONV_WIDTH), F32)],
        compiler_params=_cparams(("parallel",)),
        name="conv_module",
    )(u, buf32, w, b, lg, lb, shift)


LRU_PAD = 8
LRU_CHUNK = 256


def _lru_kernel(gate_ref, xb_ref, buf_ref, h0_ref, cw_ref, cb_ref, wa_ref, ba_ref, wx_ref, bx_ref,
                lam_ref, y_ref, nb_ref, hl_ref, pad_scr, a_scr, b_scr, h_scr, *, t_pad, t_real):
    pad_scr[0:LRU_PAD, :] = buf_ref[...]
    pad_scr[LRU_PAD:LRU_PAD + t_pad, :] = xb_ref[...]
    off = LRU_PAD - (LRU_CONV - 1)
    rc = min(LRU_CHUNK, t_pad)
    lam = lam_ref[...]
    sp = jnp.maximum(-lam, 0.0) + _log1p(jnp.exp(-jnp.abs(lam)))

    def gates(c0):
        xc = jnp.broadcast_to(cb_ref[...], (rc, LRU_WIDTH))
        win = pad_scr[pl.ds(c0, rc + LRU_PAD), :]
        for j in range(LRU_CONV):
            xc = xc + cw_ref[j:j + 1, :] * win[off + j:off + j + rc, :]
        xcb = xc.astype(BF16)
        r = _sigmoid(jnp.dot(xcb, wa_ref[...], preferred_element_type=F32) + ba_ref[...])
        i = _sigmoid(jnp.dot(xcb, wx_ref[...], preferred_element_type=F32) + bx_ref[...])
        log_a = (-LRU_C) * r * sp
        a_scr[pl.ds(c0, rc), :] = jnp.exp(log_a)
        b_scr[pl.ds(c0, rc), :] = jnp.sqrt(-_expm1_nonpos(2.0 * log_a)) * (i * xc)

    if t_pad == rc:
        gates(0)
    else:
        def gbody(ci, carry):
            gates(pl.multiple_of(ci * rc, rc))
            return carry
        lax.fori_loop(0, t_pad // rc, gbody, 0)

    row = lax.broadcasted_iota(jnp.int32, (8, LRU_WIDTH), 0)

    def sbody(gi, h):
        r0 = pl.multiple_of(gi * 8, 8)
        a = a_scr[pl.ds(r0, 8), :]
        b = b_scr[pl.ds(r0, 8), :]
        for s in (1, 2, 4):
            a_sh = jnp.where(row >= s, pltpu.roll(a, s, 0), 1.0)
            b_sh = jnp.where(row >= s, pltpu.roll(b, s, 0), 0.0)
            b = a * b_sh + b
            a = a * a_sh
        hs = a * h + b
        h_scr[pl.ds(r0, 8), :] = hs
        return hs[7:8, :]

    lax.fori_loop(0, t_pad // 8, sbody, h0_ref[...])
    y_ref[...] = (h_scr[...] * gate_ref[...]).astype(y_ref.dtype)
    nb_ref[...] = pad_scr[t_real:t_real + LRU_PAD, :]
    hl_ref[...] = h_scr[t_real - 1:t_real, :]


def _lru_block(gate, xb, buf8, h0, cw, cb, wa, ba, wx, bx, lam, layer, t_real):
    batch, t_pad, _ = gate.shape
    vec = pl.BlockSpec((1, LRU_WIDTH), lambda i: (0, 0))
    seq = pl.BlockSpec((None, t_pad, LRU_WIDTH), lambda i: (i, 0, 0))
    mat = pl.BlockSpec((None, LRU_WIDTH, LRU_WIDTH), lambda i: (layer, 0, 0))
    return pl.pallas_call(
        functools.partial(_lru_kernel, t_pad=t_pad, t_real=t_real),
        grid=(batch,),
        in_specs=[seq, seq,
                  pl.BlockSpec((None, LRU_PAD, LRU_WIDTH), lambda i: (i, 0, 0)),
                  pl.BlockSpec((None, 1, LRU_WIDTH), lambda i: (i, 0, 0)),
                  pl.BlockSpec((LRU_CONV, LRU_WIDTH), lambda i: (0, 0)),
                  vec, mat, vec, mat, vec, vec],
        out_specs=[seq,
                   pl.BlockSpec((None, LRU_PAD, LRU_WIDTH), lambda i: (i, 0, 0)),
                   pl.BlockSpec((None, 1, LRU_WIDTH), lambda i: (i, 0, 0))],
        out_shape=[jax.ShapeDtypeStruct((batch, t_pad, LRU_WIDTH), BF16),
                   jax.ShapeDtypeStruct((batch, LRU_PAD, LRU_WIDTH), F32),
                   jax.ShapeDtypeStruct((batch, 1, LRU_WIDTH), F32)],
        scratch_shapes=[pltpu.VMEM((LRU_PAD + t_pad, LRU_WIDTH), F32),
                        pltpu.VMEM((t_pad, LRU_WIDTH), F32),
                        pltpu.VMEM((t_pad, LRU_WIDTH), F32),
                        pltpu.VMEM((t_pad, LRU_WIDTH), F32)],
        compiler_params=_cparams(("parallel",)),
        name="lru_block",
    )(gate, xb, buf8, h0, cw, cb, wa, ba, wx, bx, lam)


def _out_proj_kernel(x_ref, oa_ref, oc_ref, ol_ref, w_ref, y_ref):
    acc = x_ref[...]
    acc = acc + jnp.dot(oa_ref[...], w_ref[0:ATTN_WIDTH, :], preferred_element_type=F32)
    acc = acc + jnp.dot(oc_ref[...], w_ref[ATTN_WIDTH:ATTN_WIDTH + CONV_WIDTH, :],
                        preferred_element_type=F32)
    acc = acc + jnp.dot(ol_ref[...], w_ref[ATTN_WIDTH + CONV_WIDTH:, :],
                        preferred_element_type=F32)
    y_ref[...] = acc


def _out_proj(x, oa, oc, ol, w, layer):
    m = x.shape[0]
    tm, _ = _tiles(m)
    assert m % tm == 0
    row = lambda i: (i, 0)
    return pl.pallas_call(
        _out_proj_kernel,
        grid=(m // tm,),
        in_specs=[pl.BlockSpec((tm, D_MODEL), row), pl.BlockSpec((tm, ATTN_WIDTH), row),
                  pl.BlockSpec((tm, CONV_WIDTH), row), pl.BlockSpec((tm, LRU_WIDTH), row),
                  pl.BlockSpec((None, D_MODEL, D_MODEL), lambda i: (layer, 0, 0))],
        out_specs=pl.BlockSpec((tm, D_MODEL), row),
        out_shape=jax.ShapeDtypeStruct((m, D_MODEL), F32),
        compiler_params=_cparams(("parallel",)),
        name="out_proj",
    )(x, oa, oc, ol, w)


def _ffn_step(f, x_ref, g_ref, wg_ref, wu_ref, wd_ref, y_ref, xn_scr):
    @pl.when(f == 0)
    def _():
        x = x_ref[...]
        ms = jnp.mean(x * x, axis=-1, keepdims=True)
        xn_scr[...] = (x * lax.rsqrt(ms + EPS) * g_ref[...]).astype(BF16)
        y_ref[...] = x

    xn = xn_scr[...]
    gate = jnp.dot(xn, wg_ref[...], preferred_element_type=F32)
    up = jnp.dot(xn, wu_ref[...], preferred_element_type=F32)
    hid = (_silu(gate) * up).astype(BF16)
    y_ref[...] += jnp.dot(hid, wd_ref[...], preferred_element_type=F32)


def _ffn_kernel(x_ref, g_ref, wg_ref, wu_ref, wd_ref, y_ref, xn_scr):
    _ffn_step(pl.program_id(1), x_ref, g_ref, wg_ref, wu_ref, wd_ref, y_ref, xn_scr)


def _ffn_decode_kernel(pt_ref, x_ref, g_ref, wg_ref, wu_ref, wd_ref,
                       lq1, lk1, lq2, lk2, sub_ref, q_ref, kn_ref, vn_ref, *rest,
                       lam_init, n_new, nsteps):
    del pt_ref
    g = PAGES_PER_STEP
    k_refs, v_refs = rest[:g], rest[g:2 * g]
    y_ref, o_ref, xn_scr, m_scr, l_scr, acc_scr = rest[2 * g:]
    f = pl.program_id(1)
    _ffn_step(f, x_ref, g_ref, wg_ref, wu_ref, wd_ref, y_ref, xn_scr)

    @pl.when(f < nsteps)
    def _():
        _decode_step(f, nsteps, (lq1, lk1, lq2, lk2), sub_ref, q_ref, kn_ref, vn_ref, k_refs, v_refs,
                     o_ref, m_scr, l_scr, acc_scr, lam_init=lam_init, n_new=n_new)


def _ffn_decode(x, g, wg, wu, wd, layer, page_table, lams, sub, q_rows, k_new, v_new,
                cache_k, cache_v, lam_init):
    m = x.shape[0]
    tm, tf = _tiles(m)
    d_ff = wg.shape[-1]
    batch, n_pages = page_table.shape
    n_new = k_new.shape[1]
    pg = PAGES_PER_STEP
    nrow = 2 * N_HEADS * n_new
    nsteps = n_pages // pg
    assert m % tm == 0 and d_ff % tf == 0 and n_pages % pg == 0
    assert m // tm == batch and d_ff // tf >= nsteps
    row = lambda i, f, pt: (i, 0)
    const = lambda i, f, pt: (0, 0)
    seq3 = lambda i, f, pt: (i, 0, 0)
    new_spec = pl.BlockSpec((None, n_new, N_HEADS, V_HEAD_DIM), lambda i, f, pt: (i, 0, 0, 0))

    def page_spec(r):
        return pl.BlockSpec(
            (None, None, PAGE_SIZE, N_HEADS, V_HEAD_DIM),
            lambda i, f, pt: (layer, pt[i, jnp.minimum(f, nsteps - 1) * pg + r], 0, 0, 0))

    vec = pl.BlockSpec((1, QK_HEAD_DIM), const)
    in_specs = ([pl.BlockSpec((tm, D_MODEL), row),
                 pl.BlockSpec((1, D_MODEL), const),
                 pl.BlockSpec((None, D_MODEL, tf), lambda i, f, pt: (layer, 0, f)),
                 pl.BlockSpec((None, D_MODEL, tf), lambda i, f, pt: (layer, 0, f)),
                 pl.BlockSpec((None, tf, D_MODEL), lambda i, f, pt: (layer, f, 0)),
                 vec, vec, vec, vec, pl.BlockSpec((1, V_HEAD_DIM), const),
                 pl.BlockSpec((None, nrow, V_HEAD_DIM), seq3), new_spec, new_spec]
                + [page_spec(r) for r in range(pg)] + [page_spec(r) for r in range(pg)])
    grid_spec = pltpu.PrefetchScalarGridSpec(
        num_scalar_prefetch=1,
        grid=(m // tm, d_ff // tf),
        in_specs=in_specs,
        out_specs=[pl.BlockSpec((tm, D_MODEL), row),
                   pl.BlockSpec((None, N_HEADS * n_new, V_HEAD_DIM), seq3)],
        scratch_shapes=[pltpu.VMEM((tm, D_MODEL), BF16),
                        pltpu.VMEM((nrow, 1), F32), pltpu.VMEM((nrow, 1), F32),
                        pltpu.VMEM((nrow, V_HEAD_DIM), F32)],
    )
    return pl.pallas_call(
        functools.partial(_ffn_decode_kernel, lam_init=lam_init, n_new=n_new, nsteps=nsteps),
        grid_spec=grid_spec,
        out_shape=[jax.ShapeDtypeStruct((m, D_MODEL), F32),
                   jax.ShapeDtypeStruct((batch, N_HEADS * n_new, V_HEAD_DIM), BF16)],
        compiler_params=_cparams(("parallel", "arbitrary")),
        name="ffn_decode",
    )(page_table, x, g, wg, wu, wd, *lams, sub, q_rows, k_new, v_new,
      *([cache_k] * pg), *([cache_v] * pg))


def _ffn(x, g, wg, wu, wd, layer):
    m = x.shape[0]
    tm, tf = _tiles(m)
    d_ff = wg.shape[-1]
    assert m % tm == 0 and d_ff % tf == 0
    row = lambda i, f: (i, 0)
    return pl.pallas_call(
        _ffn_kernel,
        grid=(m // tm, d_ff // tf),
        in_specs=[pl.BlockSpec((tm, D_MODEL), row),
                  pl.BlockSpec((1, D_MODEL), lambda i, f: (0, 0)),
                  pl.BlockSpec((None, D_MODEL, tf), lambda i, f: (layer, 0, f)),
                  pl.BlockSpec((None, D_MODEL, tf), lambda i, f: (layer, 0, f)),
                  pl.BlockSpec((None, tf, D_MODEL), lambda i, f: (layer, f, 0))],
        out_specs=pl.BlockSpec((tm, D_MODEL), row),
        out_shape=jax.ShapeDtypeStruct((m, D_MODEL), F32),
        scratch_shapes=[pltpu.VMEM((tm, D_MODEL), BF16)],
        compiler_params=_cparams(("parallel", "arbitrary")),
        name="ffn",
    )(x, g, wg, wu, wd)


def _rope_tables(pos):
    half = ROPE_DIM // 2
    inv_freq = ROPE_THETA ** (-jnp.arange(0, ROPE_DIM, 2, dtype=F32) / ROPE_DIM)
    ang = pos.astype(F32)[:, None] * inv_freq[None, :]
    cos, sin = jnp.cos(ang), jnp.sin(ang)
    t = pos.shape[0]
    ones = jnp.ones((t, QK_HEAD_DIM - ROPE_DIM), F32)
    zeros = jnp.zeros((t, QK_HEAD_DIM - ROPE_DIM), F32)
    zh = jnp.zeros((t, half), F32)
    rc = jnp.concatenate([cos, cos, ones], axis=1)
    rs1 = jnp.concatenate([-sin, zh, zeros], axis=1)
    rs2 = jnp.concatenate([zh, sin, zeros], axis=1)
    return tuple(jnp.tile(a, (1, MXU_COLS // QK_HEAD_DIM)) for a in (rc, rs1, rs2))


def _block_diag(w):
    h, di, dj = w.shape
    eye = jnp.eye(h, dtype=w.dtype)
    return (eye[:, None, :, None] * w[:, :, None, :]).reshape(h * di, h * dj)


def _group_mean_matrix():
    r = jnp.arange(MXU_COLS) // QK_HEAD_DIM
    return ((r[:, None] == r[None, :]).astype(F32) / QK_HEAD_DIM).astype(BF16)


def _row(a, l):
    return a[l].reshape(1, -1)


def _lam_init(l):
    return 0.8 - 0.6 * math.exp(-0.3 * l)


def _lambda_vectors(p, l):
    return tuple(_row(p[n], l) for n in ('lambda_q1', 'lambda_k1', 'lambda_q2', 'lambda_k2'))


def _rope_for(start_pos, t, rows):
    tm = _in_proj_tile(rows)
    rope = _rope_tables(start_pos + jnp.arange(t, dtype=jnp.int32))
    if t < tm:
        rope = tuple(jnp.tile(a, (tm // t, 1)) for a in rope)
    return rope


def _project(x, p, l, rope):
    gain = lambda a: jnp.tile(a[l], MXU_COLS // QK_HEAD_DIM).reshape(1, MXU_COLS)
    return _in_proj(x, _row(p['norm_mix'], l), p['w_in_bf'], l, gain(p['q_norm']),
                    gain(p['k_norm']), p['gmat'], *rope)


def _branches(u, gate, xb, conv_buf, lru_buf, h0, p, l, batch, t_real):
    m = u.shape[0]

    def seq3(a, rows):
        a = a.reshape(batch, t_real, a.shape[-1])
        pad = -t_real % rows
        return a if pad == 0 else jnp.pad(a, ((0, 0), (0, pad), (0, 0)))

    conv_w = jnp.pad(p['conv_w'][l], ((0, CONV_PAD - CONV_KERNEL), (0, 0)))
    o_conv, nb_conv = _conv_module(seq3(u, BF16_ROWS), conv_buf, conv_w, _row(p['conv_b'], l),
                                   _row(p['conv_ln_g'], l), _row(p['conv_ln_b'], l), t_real)
    o_lru, nb_lru, h_last = _lru_block(
        seq3(gate, SUBLANES), seq3(xb, SUBLANES), lru_buf, h0, p['lru_conv_w'][l],
        _row(p['lru_conv_b'], l), p['lru_wa_bd'], _row(p['lru_ba'], l), p['lru_wx_bd'],
        _row(p['lru_bx'], l), _row(p['lru_lambda'], l), l, t_real)
    states = (nb_conv[:, CONV_PAD - (CONV_KERNEL - 1):],
              nb_lru[:, LRU_PAD - (LRU_CONV - 1):],
              h_last.reshape(batch, LRU_WIDTH))
    return (o_conv[:, :t_real].reshape(m, CONV_WIDTH), o_lru[:, :t_real].reshape(m, LRU_WIDTH),
            states)


def _decode_queries(q, batch, t):
    q5 = q.reshape(batch, t, N_HEADS, 2, QK_HEAD_DIM)
    zero = jnp.zeros_like(q5[:, :, :, 0])
    q_rows = jnp.stack([jnp.concatenate([q5[:, :, :, 0], zero], axis=-1),
                        jnp.concatenate([zero, q5[:, :, :, 1]], axis=-1)], axis=1)
    return q_rows.transpose(0, 1, 3, 2, 4).reshape(batch, 2 * N_HEADS * t, V_HEAD_DIM)


def kernel(x_prompt, x_sample, cache_k, cache_v, state_conv, state_lru_conv, state_lru_h, page_table,
           norm_mix, w_in, q_norm, k_norm, lambda_q1, lambda_k1, lambda_q2, lambda_k2, subln,
           conv_w, conv_b, conv_ln_g, conv_ln_b, lru_conv_w, lru_conv_b, lru_wa, lru_ba, lru_wx, lru_bx,
           lru_lambda, w_out, norm_ffn, w_ffn_gate, w_ffn_up, w_ffn_down):
    p = dict(norm_mix=norm_mix, q_norm=q_norm, k_norm=k_norm,
             lambda_q1=lambda_q1, lambda_k1=lambda_k1, lambda_q2=lambda_q2, lambda_k2=lambda_k2,
             subln=subln, conv_w=conv_w, conv_b=conv_b, conv_ln_g=conv_ln_g, conv_ln_b=conv_ln_b,
             lru_conv_w=lru_conv_w, lru_conv_b=lru_conv_b, lru_ba=lru_ba, lru_bx=lru_bx,
             lru_lambda=lru_lambda, norm_ffn=norm_ffn)
    p['w_in_bf'] = w_in.astype(BF16)
    p['w_out_bf'] = w_out.astype(BF16)
    p['w_gate_bf'] = w_ffn_gate.astype(BF16)
    p['w_up_bf'] = w_ffn_up.astype(BF16)
    p['w_down_bf'] = w_ffn_down.astype(BF16)
    p['lru_wa_bd'] = jax.vmap(_block_diag)(lru_wa).astype(BF16)
    p['lru_wx_bd'] = jax.vmap(_block_diag)(lru_wx).astype(BF16)
    p['gmat'] = _group_mean_matrix()

    depth = w_in.shape[0]
    ffn_w = (p['w_gate_bf'], p['w_up_bf'], p['w_down_bf'])

    bp, tp, _ = x_prompt.shape
    bs, ts, _ = x_sample.shape
    mp, ms = bp * tp, bs * ts
    past_len = page_table.shape[1] * PAGE_SIZE
    rope_p = _rope_for(0, tp, mp)
    rope_s = _rope_for(past_len, ts, ms)
    xp = x_prompt.reshape(mp, D_MODEL)
    xs = x_sample.reshape(ms, D_MODEL)
    outs_p, outs_s = [], []
    for l in range(depth):
        lam_init = _lam_init(l)
        lams = _lambda_vectors(p, l)
        sub = _row(p['subln'], l)

        q, k, kb, v, vb, u, gate, xb = _project(xp, p, l, rope_p)
        o_attn = _attn_prompt(lams, sub, q, kb, vb, lam_init, bp, tp)
        o_conv, o_lru, st = _branches(
            u, gate, xb, jnp.zeros((bp, CONV_PAD, CONV_WIDTH), F32),
            jnp.zeros((bp, LRU_PAD, LRU_WIDTH), F32), jnp.zeros((bp, 1, LRU_WIDTH), F32),
            p, l, bp, tp)
        xp = _out_proj(xp, o_attn, o_conv, o_lru, p['w_out_bf'], l)
        outs_p.append((k.reshape(bp, tp, N_HEADS, V_HEAD_DIM),
                       v.reshape(bp, tp, N_HEADS, V_HEAD_DIM)) + st)

        q, k, kb, v, vb, u, gate, xb = _project(xs, p, l, rope_s)
        k_new = k.reshape(bs, ts, N_HEADS, V_HEAD_DIM)
        v_new = v.reshape(bs, ts, N_HEADS, V_HEAD_DIM)
        xp, o = _ffn_decode(xp, _row(p['norm_ffn'], l), *ffn_w, l, page_table, lams, sub,
                            _decode_queries(q, bs, ts), k_new, v_new, cache_k, cache_v, lam_init)
        o_attn = o.reshape(bs, N_HEADS, ts, V_HEAD_DIM).transpose(0, 2, 1, 3).reshape(ms, ATTN_WIDTH)
        o_conv, o_lru, st = _branches(
            u, gate, xb,
            jnp.pad(state_conv[l], ((0, 0), (CONV_PAD - (CONV_KERNEL - 1), 0), (0, 0))),
            jnp.pad(state_lru_conv[l], ((0, 0), (LRU_PAD - (LRU_CONV - 1), 0), (0, 0))),
            state_lru_h[l].reshape(bs, 1, LRU_WIDTH), p, l, bs, ts)
        xs = _out_proj(xs, o_attn, o_conv, o_lru, p['w_out_bf'], l)
        xs = _ffn(xs, _row(p['norm_ffn'], l), *ffn_w, l)
        outs_s.append((k_new, v_new) + st)

    stack = lambda outs: tuple(jnp.stack([o[i] for o in outs]) for i in range(5))
    k_p, v_p, cb_p, lcb_p, h_p = stack(outs_p)
    k_s, v_s, cb_s, lcb_s, h_s = stack(outs_s)
    return (xp.reshape(bp, tp, D_MODEL), xs.reshape(bs, ts, D_MODEL),
            k_p, v_p, cb_p, lcb_p, h_p, k_s, v_s, cb_s, lcb_s, h_s)
```

```python
import functools
import math

import jax
import jax.numpy as jnp
from jax import lax
from jax.experimental import pallas as pl
from jax.experimental.pallas import tpu as pltpu

F32 = jnp.float32
BF16 = jnp.bfloat16

D_MODEL = 2048
N_HEADS = 8
V_HEAD_DIM = 128
QK_HEAD_DIM = 64
ROPE_DIM = 16
ROPE_THETA = 500000.0
ATTN_SCALE = 1.0 / math.sqrt(QK_HEAD_DIM)
LOG2_E = 1.4426950408889634
Q_SCALE = ATTN_SCALE * LOG2_E
ATTN_WIDTH = N_HEADS * V_HEAD_DIM
CONV_WIDTH = 512
LRU_WIDTH = 512
CONV_KERNEL = 31
LRU_CONV = 4
LRU_C = 8.0
EPS = 1e-6
NEG_INF = -1e30
PAGE_SIZE = 128
PAGES_PER_STEP = 6
GROUP_COLS = 1024
MXU_COLS = 256
BF16_ROWS = 16
VMEM_LIMIT = 56 * 1024 * 1024


FFN_TILE = 512


def _tiles(m):
    return min(512, m), FFN_TILE


def _in_proj_tile(m):
    return min(256, m)


def _cparams(sem):
    return pltpu.CompilerParams(dimension_semantics=sem, vmem_limit_bytes=VMEM_LIMIT)


def _sigmoid(x):
    return 1.0 / (1.0 + jnp.exp(-x))


def _silu(x):
    return x * _sigmoid(x)


def _gelu_tanh(x):
    c = math.sqrt(2.0 / math.pi)
    return x * (0.5 * (1.0 + jnp.tanh(c * (x + 0.044715 * (x * x * x)))))


def _log1p(z):
    w = 1.0 + z
    small = w == 1.0
    return jnp.where(small, z, jnp.log(w) * z / jnp.where(small, 1.0, w - 1.0))


def _expm1_nonpos(x):
    u = jnp.exp(x)
    direct = (u == 1.0) | (x < -20.0)
    ratio = (u - 1.0) * x / jnp.where(direct, 1.0, jnp.log(u))
    return jnp.where(u == 1.0, x, jnp.where(x < -20.0, u - 1.0, ratio))


def _lambda_full(lq1, lk1, lq2, lk2, lam_init):
    s1 = jnp.sum(lq1[...] * lk1[...], axis=-1, keepdims=True)
    s2 = jnp.sum(lq2[...] * lk2[...], axis=-1, keepdims=True)
    return jnp.exp(s1) - jnp.exp(s2) + lam_init


def _qk_chunk(y, gain, gmat, rc, rs1, rs2, scale):
    ms = jnp.dot((y * y).astype(BF16), gmat, preferred_element_type=F32)
    yn = y * lax.rsqrt(ms + EPS) * gain
    rot = yn * rc + pltpu.roll(yn, MXU_COLS - ROPE_DIM // 2, 1) * rs1 \
        + pltpu.roll(yn, ROPE_DIM // 2, 1) * rs2
    return rot * scale if scale != 1.0 else rot


def _in_proj_kernel(x_ref, g_ref, w_ref, qn_ref, kn_ref, gmat_ref, rc_ref, rs1_ref, rs2_ref,
                    q_ref, k_ref, kb_ref, v_ref, vb_ref, u_ref, gate_ref, xb_ref, xn_scr):
    x = x_ref[...]
    ms = jnp.mean(x * x, axis=-1, keepdims=True)
    xn_scr[...] = (x * lax.rsqrt(ms + EPS) * g_ref[...]).astype(BF16)

    def cols(c):
        return slice(c * MXU_COLS, (c + 1) * MXU_COLS)

    def proj(group):
        w = w_ref[:, group * GROUP_COLS:(group + 1) * GROUP_COLS]
        return jnp.dot(xn_scr[...], w, preferred_element_type=F32)

    y = proj(0)
    for c in range(GROUP_COLS // MXU_COLS):
        r = _qk_chunk(y[:, cols(c)], qn_ref[...], gmat_ref[...], rc_ref[...], rs1_ref[...],
                      rs2_ref[...], Q_SCALE)
        q_ref[:, cols(c)] = r.astype(BF16)

    y = proj(1)
    for c in range(GROUP_COLS // MXU_COLS):
        r = _qk_chunk(y[:, cols(c)], kn_ref[...], gmat_ref[...], rc_ref[...], rs1_ref[...],
                      rs2_ref[...], 1.0)
        k_ref[:, cols(c)] = r
        kb_ref[:, cols(c)] = r.astype(BF16)

    y = proj(2)
    v_ref[...] = y
    vb_ref[...] = y.astype(BF16)

    y = proj(3)
    u_ref[...] = y[:, :CONV_WIDTH] * _sigmoid(y[:, CONV_WIDTH:])

    y = proj(4)
    gate_ref[...] = _gelu_tanh(y[:, :LRU_WIDTH])
    xb_ref[...] = y[:, LRU_WIDTH:]


def _in_proj(x, g, w, layer, qn, kn, gmat, rc, rs1, rs2):
    m = x.shape[0]
    tm = _in_proj_tile(m)
    assert m % tm == 0 and rc.shape[0] % tm == 0
    nt = m // tm
    nrope = rc.shape[0] // tm
    row = lambda i: (i, 0)
    const = lambda i: (0, 0)
    rope = lambda i: (i % nrope, 0)
    out_shape = [
        jax.ShapeDtypeStruct((m, ATTN_WIDTH), BF16),
        jax.ShapeDtypeStruct((m, ATTN_WIDTH), F32),
        jax.ShapeDtypeStruct((m, ATTN_WIDTH), BF16),
        jax.ShapeDtypeStruct((m, ATTN_WIDTH), F32),
        jax.ShapeDtypeStruct((m, ATTN_WIDTH), BF16),
        jax.ShapeDtypeStruct((m, CONV_WIDTH), F32),
        jax.ShapeDtypeStruct((m, LRU_WIDTH), F32),
        jax.ShapeDtypeStruct((m, LRU_WIDTH), F32),
    ]
    out_specs = [
        pl.BlockSpec((tm, ATTN_WIDTH), row), pl.BlockSpec((tm, ATTN_WIDTH), row),
        pl.BlockSpec((tm, ATTN_WIDTH), row), pl.BlockSpec((tm, ATTN_WIDTH), row),
        pl.BlockSpec((tm, ATTN_WIDTH), row), pl.BlockSpec((tm, CONV_WIDTH), row),
        pl.BlockSpec((tm, LRU_WIDTH), row), pl.BlockSpec((tm, LRU_WIDTH), row),
    ]
    in_specs = [
        pl.BlockSpec((tm, D_MODEL), row),
        pl.BlockSpec((1, D_MODEL), const),
        pl.BlockSpec((None, D_MODEL, 5 * GROUP_COLS), lambda i: (layer, 0, 0),
                     pipeline_mode=pl.Buffered(1)),
        pl.BlockSpec((1, MXU_COLS), const), pl.BlockSpec((1, MXU_COLS), const),
        pl.BlockSpec((MXU_COLS, MXU_COLS), const),
        pl.BlockSpec((tm, MXU_COLS), rope), pl.BlockSpec((tm, MXU_COLS), rope),
        pl.BlockSpec((tm, MXU_COLS), rope),
    ]
    return pl.pallas_call(
        _in_proj_kernel,
        grid=(nt,),
        in_specs=in_specs,
        out_specs=out_specs,
        out_shape=out_shape,
        scratch_shapes=[pltpu.VMEM((tm, D_MODEL), BF16)],
        compiler_params=_cparams(("parallel",)),
        name="in_proj",
    )(x, g, w, qn, kn, gmat, rc, rs1, rs2)


HEADS_PER_STEP = 2


def _attn_prompt_kernel(lq1, lk1, lq2, lk2, sub_ref, q_ref, k_ref, v_ref, o_ref,
                        qt_scr, vt_scr, m_scr, acc_scr, sa_scr, sb_scr, *, lam_init, seq, tq):
    lam = _lambda_full(lq1, lk1, lq2, lk2, lam_init)
    heads = range(HEADS_PER_STEP)
    hcols = lambda h: slice(h * V_HEAD_DIM, (h + 1) * V_HEAD_DIM)
    for c in range(seq // tq):
        sl = slice(c * tq, (c + 1) * tq)
        qt_scr[:, sl] = q_ref[sl, :].astype(F32).T.astype(BF16)
        vt = v_ref[sl, :].astype(F32).T.astype(BF16)
        for h in heads:
            vt_scr[h, 0:V_HEAD_DIM, sl] = vt[hcols(h)]
    first_row = lax.broadcasted_iota(jnp.int32, (BF16_ROWS, seq), 0) == 0
    for h in heads:
        vt_scr[h, V_HEAD_DIM:V_HEAD_DIM + BF16_ROWS, :] = \
            jnp.where(first_row, 1.0, 0.0).astype(BF16)

    key = lax.broadcasted_iota(jnp.int32, (tq, 2 * tq), 0)
    col = lax.broadcasted_iota(jnp.int32, (tq, 2 * tq), 1)
    causal = key <= jnp.where(col >= tq, col - tq, col)
    zeros = jnp.zeros((QK_HEAD_DIM, tq), BF16)

    bufs = (sa_scr, sb_scr)

    def produce(qs, buf, start):
        for h in heads:
            buf[h] = jnp.dot(k_ref[pl.ds(start, tq), hcols(h)], qs[h],
                             preferred_element_type=F32)

    def consume(buf, start, masked):
        s = [buf[h] for h in heads]
        if masked:
            s = [jnp.where(causal, sh, NEG_INF) for sh in s]
        m_new = [jnp.maximum(m_scr[h], jnp.max(s[h], axis=0, keepdims=True)) for h in heads]
        for h in heads:
            alpha = jnp.exp2(m_scr[h] - m_new[h])
            p = jnp.exp2(s[h] - m_new[h]).astype(BF16)
            pv = jnp.dot(vt_scr[h, :, pl.ds(start, tq)], p, preferred_element_type=F32)
            acc_scr[h] = alpha * acc_scr[h] + pv
            m_scr[h] = m_new[h]

    for qi in range(seq // tq):
        qs = []
        for h in heads:
            qt = qt_scr[hcols(h), qi * tq:(qi + 1) * tq]
            qs.append(jnp.concatenate(
                [jnp.concatenate([qt[:QK_HEAD_DIM], zeros], axis=0),
                 jnp.concatenate([zeros, qt[QK_HEAD_DIM:]], axis=0)], axis=1))
        m_scr[...] = jnp.full(m_scr.shape, NEG_INF, F32)
        acc_scr[...] = jnp.zeros(acc_scr.shape, F32)

        produce(qs, bufs[0], 0)
        npairs = qi // 2

        def pair(i, carry):
            base = pl.multiple_of(i * (2 * tq), 2 * tq)
            mid = pl.multiple_of(base + tq, tq)
            produce(qs, bufs[1], mid)
            consume(bufs[0], base, False)
            produce(qs, bufs[0], pl.multiple_of(base + 2 * tq, 2 * tq))
            consume(bufs[1], mid, False)
            return carry

        if npairs > 0:
            lax.fori_loop(0, npairs, pair, 0)
        last = 0
        if qi % 2:
            produce(qs, bufs[1], qi * tq)
            consume(bufs[0], (qi - 1) * tq, False)
            last = 1
        consume(bufs[last], qi * tq, True)

        for h in heads:
            acc = acc_scr[h]
            on = acc[0:V_HEAD_DIM] / acc[V_HEAD_DIM:V_HEAD_DIM + 1]
            d = on[:, :tq] - lam * on[:, tq:]
            ms = jnp.mean(d * d, axis=0, keepdims=True)
            y = d * lax.rsqrt(ms + EPS) * sub_ref[...] * (1.0 - lam_init)
            o_ref[qi * tq:(qi + 1) * tq, hcols(h)] = y.T.astype(o_ref.dtype)


def _attn_prompt(lams, sub, q, kb, vb, lam_init, batch, seq, tq=256):
    hp = HEADS_PER_STEP
    assert seq % tq == 0 and N_HEADS % hp == 0
    vec = pl.BlockSpec((1, QK_HEAD_DIM), lambda b, h: (0, 0))
    blk = pl.BlockSpec((seq, hp * V_HEAD_DIM), lambda b, h: (b, h))
    sub_cols = jnp.broadcast_to(sub.reshape(V_HEAD_DIM, 1), (V_HEAD_DIM, tq))
    return pl.pallas_call(
        functools.partial(_attn_prompt_kernel, lam_init=lam_init, seq=seq, tq=tq),
        grid=(batch, N_HEADS // hp),
        in_specs=[vec, vec, vec, vec, pl.BlockSpec((V_HEAD_DIM, tq), lambda b, h: (0, 0)),
                  blk, blk, blk],
        out_specs=blk,
        out_shape=jax.ShapeDtypeStruct((batch * seq, ATTN_WIDTH), BF16),
        scratch_shapes=[pltpu.VMEM((hp * V_HEAD_DIM, seq), BF16),
                        pltpu.VMEM((hp, V_HEAD_DIM + BF16_ROWS, seq), BF16),
                        pltpu.VMEM((hp, 1, 2 * tq), F32),
                        pltpu.VMEM((hp, V_HEAD_DIM + BF16_ROWS, 2 * tq), F32),
                        pltpu.VMEM((hp, tq, 2 * tq), F32),
                        pltpu.VMEM((hp, tq, 2 * tq), F32)],
        compiler_params=_cparams(("parallel", "parallel")),
        name="attn_prompt",
    )(*lams, sub_cols, q, kb, vb)


def _decode_step(step, nsteps, lams, sub_ref, q_ref, kn_ref, vn_ref, k_refs, v_refs, o_ref,
                 m_scr, l_scr, acc_scr, *, lam_init, n_new, n_pages):
    g = PAGES_PER_STEP
    lq1, lk1, lq2, lk2 = lams
    nrow = 2 * N_HEADS * n_new
    q = q_ref[...]
    nt_dims = (((1,), (1,)), ((), ()))

    def head_of_row(shape):
        r = lax.broadcasted_iota(jnp.int32, shape, 0)
        return (r % (N_HEADS * n_new)) // n_new, r % n_new

    def update(s, vmat):
        m = m_scr[...]
        m_new = jnp.maximum(m, jnp.max(s, axis=-1, keepdims=True))
        alpha = jnp.exp2(m - m_new)
        p = jnp.exp2(s - m_new)
        l_scr[...] = alpha * l_scr[...] + jnp.sum(p, axis=-1, keepdims=True)
        acc_scr[...] = alpha * acc_scr[...] + jnp.dot(p.astype(BF16), vmat,
                                                      preferred_element_type=F32)
        m_scr[...] = m_new

    @pl.when(step == 0)
    def _():
        m_scr[...] = jnp.full(m_scr.shape, NEG_INF, F32)
        l_scr[...] = jnp.zeros(l_scr.shape, F32)
        acc_scr[...] = jnp.zeros(acc_scr.shape, F32)
        kn = kn_ref[...].reshape(n_new * N_HEADS, V_HEAD_DIM).astype(BF16)
        vn = vn_ref[...].reshape(n_new * N_HEADS, V_HEAD_DIM).astype(BF16)
        s = lax.dot_general(q, kn, nt_dims, preferred_element_type=F32)
        shape = s.shape
        hrow, trow = head_of_row(shape)
        c = lax.broadcasted_iota(jnp.int32, shape, 1)
        ok = (hrow == c % N_HEADS) & (c // N_HEADS <= trow)
        update(jnp.where(ok, s, NEG_INF), vn)

    ncol = PAGE_SIZE * N_HEADS
    hrow, _ = head_of_row((nrow, ncol))
    ok = hrow == lax.broadcasted_iota(jnp.int32, (nrow, ncol), 1) % N_HEADS
    s_list = []
    m_new = m_scr[...]
    for p in range(g):
        kp = k_refs[p][...].reshape(ncol, V_HEAD_DIM).astype(BF16)
        s = lax.dot_general(q, kp, nt_dims, preferred_element_type=F32)
        s = jnp.where(ok, s, NEG_INF)
        if n_pages % g:
            s = jnp.where(step * g + p < n_pages, s, NEG_INF)
        s_list.append(s)
        m_new = jnp.maximum(m_new, jnp.max(s, axis=-1, keepdims=True))
    alpha = jnp.exp2(m_scr[...] - m_new)
    l = alpha * l_scr[...]
    acc = alpha * acc_scr[...]
    for p in range(g):
        pr = jnp.exp2(s_list[p] - m_new)
        l = l + jnp.sum(pr, axis=-1, keepdims=True)
        vp = v_refs[p][...].reshape(ncol, V_HEAD_DIM).astype(BF16)
        acc = acc + jnp.dot(pr.astype(BF16), vp, preferred_element_type=F32)
    m_scr[...] = m_new
    l_scr[...] = l
    acc_scr[...] = acc

    @pl.when(step == nsteps - 1)
    def _():
        lam = _lambda_full(lq1, lk1, lq2, lk2, lam_init)
        on = acc_scr[...] / l_scr[...]
        half = N_HEADS * n_new
        o = on[:half] - lam * on[half:]
        ms = jnp.mean(o * o, axis=-1, keepdims=True)
        o = o * lax.rsqrt(ms + EPS) * sub_ref[...] * (1.0 - lam_init)
        o_ref[...] = o.astype(o_ref.dtype)


CONV_PAD = 32
CONV_CHUNK = 32
CONV_UNROLL = 4
SUBLANES = 8


def _shift_matrix(nwin):
    span = nwin - SUBLANES
    r = jnp.arange((SUBLANES - 1) * span)
    src = r % span + r // span + 1
    one = (src[:, None] == jnp.arange(nwin)[None, :]).astype(BF16)
    return jnp.concatenate([one, one, one], axis=1)


def _conv_kernel(u_ref, buf_ref, w_ref, b_ref, lg_ref, lb_ref, shift_ref, o_ref, nb_ref, pad_scr,
                 *, t_pad, t_real):
    pad_scr[0:CONV_PAD, :] = buf_ref[...]
    pad_scr[CONV_PAD:CONV_PAD + t_pad, :] = u_ref[...]
    off = CONV_PAD - (CONV_KERNEL - 1)
    rc = min(CONV_CHUNK, t_pad)
    nwin = rc + CONV_PAD
    span = nwin - SUBLANES

    def chunk(c0):
        acc = jnp.broadcast_to(b_ref[...], (rc, CONV_WIDTH))
        win = pad_scr[pl.ds(c0, nwin), :]
        hi = win.astype(BF16)
        r1 = win - hi.astype(F32)
        mid = r1.astype(BF16)
        lo = (r1 - mid.astype(F32)).astype(BF16)
        shifted = jnp.dot(shift_ref[...], jnp.concatenate([hi, mid, lo], axis=0),
                          preferred_element_type=F32)
        for s in range(SUBLANES):
            taps = [j for j in range(CONV_KERNEL) if (off + j) % SUBLANES == s]
            sh = win if s == 0 else shifted[(s - 1) * span:s * span]
            for j in taps:
                a0 = off + j - s
                acc = acc + w_ref[j:j + 1, :] * sh[a0:a0 + rc, :]
        mu = jnp.mean(acc, axis=-1, keepdims=True)
        d = acc - mu
        var = jnp.mean(d * d, axis=-1, keepdims=True)
        y = d * lax.rsqrt(var + EPS) * lg_ref[...] + lb_ref[...]
        o_ref[pl.ds(c0, rc), :] = _silu(y).astype(o_ref.dtype)

    if t_pad == rc:
        chunk(0)
    else:
        def body(ci, carry):
            chunk(pl.multiple_of(ci * rc, rc))
            return carry
        lax.fori_loop(0, t_pad // rc, body, 0, unroll=CONV_UNROLL)
    nb_ref[...] = pad_scr[t_real:t_real + CONV_PAD, :]


def _conv_module(u, buf32, w, b, lg, lb, t_real):
    batch, t_pad, _ = u.shape
    vec = pl.BlockSpec((1, CONV_WIDTH), lambda i: (0, 0))
    shift = _shift_matrix(min(CONV_CHUNK, t_pad) + CONV_PAD)
    return pl.pallas_call(
        functools.partial(_conv_kernel, t_pad=t_pad, t_real=t_real),
        grid=(batch,),
        in_specs=[pl.BlockSpec((None, t_pad, CONV_WIDTH), lambda i: (i, 0, 0)),
                  pl.BlockSpec((None, CONV_PAD, CONV_WIDTH), lambda i: (i, 0, 0)),
                  pl.BlockSpec((CONV_PAD, CONV_WIDTH), lambda i: (0, 0)),
                  vec, vec, vec, pl.BlockSpec(shift.shape, lambda i: (0, 0))],
        out_specs=[pl.BlockSpec((None, t_pad, CONV_WIDTH), lambda i: (i, 0, 0)),
                   pl.BlockSpec((None, CONV_PAD, CONV_WIDTH), lambda i: (i, 0, 0))],
        out_shape=[jax.ShapeDtypeStruct((batch, t_pad, CONV_WIDTH), BF16),
                   jax.ShapeDtypeStruct((batch, CONV_PAD, CONV_WIDTH), F32)],
        scratch_shapes=[pltpu.VMEM((CONV_PAD + t_pad, CONV_WIDTH), F32)],
        compiler_params=_cparams(("parallel",)),
        name="conv_module",
    )(u, buf32, w, b, lg, lb, shift)


LRU_PAD = 8
LRU_CHUNK = 256


def _lru_kernel(gate_ref, xb_ref, buf_ref, h0_ref, cw_ref, cb_ref, wa_ref, ba_ref, wx_ref, bx_ref,
                lam_ref, y_ref, nb_ref, hl_ref, pad_scr, a_scr, b_scr, h_scr, *, t_pad, t_real):
    pad_scr[0:LRU_PAD, :] = buf_ref[...]
    pad_scr[LRU_PAD:LRU_PAD + t_pad, :] = xb_ref[...]
    off = LRU_PAD - (LRU_CONV - 1)
    rc = min(LRU_CHUNK, t_pad)
    lam = lam_ref[...]
    sp = jnp.maximum(-lam, 0.0) + _log1p(jnp.exp(-jnp.abs(lam)))

    def gates(c0):
        xc = jnp.broadcast_to(cb_ref[...], (rc, LRU_WIDTH))
        win = pad_scr[pl.ds(c0, rc + LRU_PAD), :]
        for j in range(LRU_CONV):
            xc = xc + cw_ref[j:j + 1, :] * win[off + j:off + j + rc, :]
        xcb = xc.astype(BF16)
        r = _sigmoid(jnp.dot(xcb, wa_ref[...], preferred_element_type=F32) + ba_ref[...])
        i = _sigmoid(jnp.dot(xcb, wx_ref[...], preferred_element_type=F32) + bx_ref[...])
        log_a = (-LRU_C) * r * sp
        a_scr[pl.ds(c0, rc), :] = jnp.exp(log_a)
        b_scr[pl.ds(c0, rc), :] = jnp.sqrt(-_expm1_nonpos(2.0 * log_a)) * (i * xc)

    if t_pad == rc:
        gates(0)
    else:
        def gbody(ci, carry):
            gates(pl.multiple_of(ci * rc, rc))
            return carry
        lax.fori_loop(0, t_pad // rc, gbody, 0)

    row = lax.broadcasted_iota(jnp.int32, (8, LRU_WIDTH), 0)

    def sbody(gi, h):
        r0 = pl.multiple_of(gi * 8, 8)
        a = a_scr[pl.ds(r0, 8), :]
        b = b_scr[pl.ds(r0, 8), :]
        for s in (1, 2, 4):
            a_sh = jnp.where(row >= s, pltpu.roll(a, s, 0), 1.0)
            b_sh = jnp.where(row >= s, pltpu.roll(b, s, 0), 0.0)
            b = a * b_sh + b
            a = a * a_sh
        hs = a * h + b
        h_scr[pl.ds(r0, 8), :] = hs
        return hs[7:8, :]

    lax.fori_loop(0, t_pad // 8, sbody, h0_ref[...])
    y_ref[...] = (h_scr[...] * gate_ref[...]).astype(y_ref.dtype)
    nb_ref[...] = pad_scr[t_real:t_real + LRU_PAD, :]
    hl_ref[...] = h_scr[t_real - 1:t_real, :]


def _lru_block(gate, xb, buf8, h0, cw, cb, wa, ba, wx, bx, lam, layer, t_real):
    batch, t_pad, _ = gate.shape
    vec = pl.BlockSpec((1, LRU_WIDTH), lambda i: (0, 0))
    seq = pl.BlockSpec((None, t_pad, LRU_WIDTH), lambda i: (i, 0, 0))
    mat = pl.BlockSpec((None, LRU_WIDTH, LRU_WIDTH), lambda i: (layer, 0, 0))
    return pl.pallas_call(
        functools.partial(_lru_kernel, t_pad=t_pad, t_real=t_real),
        grid=(batch,),
        in_specs=[seq, seq,
                  pl.BlockSpec((None, LRU_PAD, LRU_WIDTH), lambda i: (i, 0, 0)),
                  pl.BlockSpec((None, 1, LRU_WIDTH), lambda i: (i, 0, 0)),
                  pl.BlockSpec((LRU_CONV, LRU_WIDTH), lambda i: (0, 0)),
                  vec, mat, vec, mat, vec, vec],
        out_specs=[seq,
                   pl.BlockSpec((None, LRU_PAD, LRU_WIDTH), lambda i: (i, 0, 0)),
                   pl.BlockSpec((None, 1, LRU_WIDTH), lambda i: (i, 0, 0))],
        out_shape=[jax.ShapeDtypeStruct((batch, t_pad, LRU_WIDTH), BF16),
                   jax.ShapeDtypeStruct((batch, LRU_PAD, LRU_WIDTH), F32),
                   jax.ShapeDtypeStruct((batch, 1, LRU_WIDTH), F32)],
        scratch_shapes=[pltpu.VMEM((LRU_PAD + t_pad, LRU_WIDTH), F32),
                        pltpu.VMEM((t_pad, LRU_WIDTH), F32),
                        pltpu.VMEM((t_pad, LRU_WIDTH), F32),
                        pltpu.VMEM((t_pad, LRU_WIDTH), F32)],
        compiler_params=_cparams(("parallel",)),
        name="lru_block",
    )(gate, xb, buf8, h0, cw, cb, wa, ba, wx, bx, lam)


def _out_proj_kernel(x_ref, oa_ref, oc_ref, ol_ref, w_ref, y_ref):
    acc = x_ref[...]
    acc = acc + jnp.dot(oa_ref[...], w_ref[0:ATTN_WIDTH, :], preferred_element_type=F32)
    acc = acc + jnp.dot(oc_ref[...], w_ref[ATTN_WIDTH:ATTN_WIDTH + CONV_WIDTH, :],
                        preferred_element_type=F32)
    acc = acc + jnp.dot(ol_ref[...], w_ref[ATTN_WIDTH + CONV_WIDTH:, :],
                        preferred_element_type=F32)
    y_ref[...] = acc


def _out_proj(x, oa, oc, ol, w, layer):
    m = x.shape[0]
    tm, _ = _tiles(m)
    assert m % tm == 0
    row = lambda i: (i, 0)
    return pl.pallas_call(
        _out_proj_kernel,
        grid=(m // tm,),
        in_specs=[pl.BlockSpec((tm, D_MODEL), row), pl.BlockSpec((tm, ATTN_WIDTH), row),
                  pl.BlockSpec((tm, CONV_WIDTH), row), pl.BlockSpec((tm, LRU_WIDTH), row),
                  pl.BlockSpec((None, D_MODEL, D_MODEL), lambda i: (layer, 0, 0))],
        out_specs=pl.BlockSpec((tm, D_MODEL), row),
        out_shape=jax.ShapeDtypeStruct((m, D_MODEL), F32),
        compiler_params=_cparams(("parallel",)),
        name="out_proj",
    )(x, oa, oc, ol, w)


def _ffn_step(f, x_ref, g_ref, wg_ref, wu_ref, wd_ref, y_ref, xn_scr):
    @pl.when(f == 0)
    def _():
        x = x_ref[...]
        ms = jnp.mean(x * x, axis=-1, keepdims=True)
        xn_scr[...] = (x * lax.rsqrt(ms + EPS) * g_ref[...]).astype(BF16)
        y_ref[...] = x

    xn = xn_scr[...]
    gate = jnp.dot(xn, wg_ref[...], preferred_element_type=F32)
    up = jnp.dot(xn, wu_ref[...], preferred_element_type=F32)
    hid = (_silu(gate) * up).astype(BF16)
    y_ref[...] += jnp.dot(hid, wd_ref[...], preferred_element_type=F32)


def _ffn_kernel(x_ref, g_ref, wg_ref, wu_ref, wd_ref, y_ref, xn_scr):
    _ffn_step(pl.program_id(1), x_ref, g_ref, wg_ref, wu_ref, wd_ref, y_ref, xn_scr)


def _ffn_decode_kernel(pt_ref, x_ref, g_ref, wg_ref, wu_ref, wd_ref,
                       lq1, lk1, lq2, lk2, sub_ref, q_ref, kn_ref, vn_ref, *rest,
                       lam_init, n_new, nsteps, n_pages):
    del pt_ref
    g = PAGES_PER_STEP
    k_refs, v_refs = rest[:g], rest[g:2 * g]
    y_ref, o_ref, xn_scr, m_scr, l_scr, acc_scr = rest[2 * g:]
    f = pl.program_id(1)
    _ffn_step(f, x_ref, g_ref, wg_ref, wu_ref, wd_ref, y_ref, xn_scr)

    @pl.when(f < nsteps)
    def _():
        _decode_step(f, nsteps, (lq1, lk1, lq2, lk2), sub_ref, q_ref, kn_ref, vn_ref, k_refs, v_refs,
                     o_ref, m_scr, l_scr, acc_scr, lam_init=lam_init, n_new=n_new,
                     n_pages=n_pages)


def _ffn_decode(x, g, wg, wu, wd, layer, page_table, lams, sub, q_rows, k_new, v_new,
                cache_k, cache_v, lam_init):
    m = x.shape[0]
    tm, tf = _tiles(m)
    d_ff = wd.shape[1]
    assert wg.shape[1:] == (d_ff // tf, D_MODEL, tf)
    batch, n_pages = page_table.shape
    n_new = k_new.shape[1]
    pg = PAGES_PER_STEP
    nrow = 2 * N_HEADS * n_new
    nsteps = -(-n_pages // pg)
    assert m % tm == 0 and d_ff % tf == 0
    assert m // tm == batch and d_ff // tf >= nsteps
    row = lambda i, f, pt: (i, 0)
    const = lambda i, f, pt: (0, 0)
    seq3 = lambda i, f, pt: (i, 0, 0)
    new_spec = pl.BlockSpec((None, n_new, N_HEADS, V_HEAD_DIM), lambda i, f, pt: (i, 0, 0, 0))

    def page_spec(r):
        def index(i, f, pt):
            slot = jnp.minimum(jnp.minimum(f, nsteps - 1) * pg + r, n_pages - 1)
            return (layer, pt[i, slot], 0, 0, 0)
        return pl.BlockSpec((None, None, PAGE_SIZE, N_HEADS, V_HEAD_DIM), index)

    vec = pl.BlockSpec((1, QK_HEAD_DIM), const)
    in_specs = ([pl.BlockSpec((tm, D_MODEL), row),
                 pl.BlockSpec((1, D_MODEL), const),
                 pl.BlockSpec((None, None, D_MODEL, tf), lambda i, f, pt: (layer, f, 0, 0)),
                 pl.BlockSpec((None, None, D_MODEL, tf), lambda i, f, pt: (layer, f, 0, 0)),
                 pl.BlockSpec((None, tf, D_MODEL), lambda i, f, pt: (layer, f, 0)),
                 vec, vec, vec, vec, pl.BlockSpec((1, V_HEAD_DIM), const),
                 pl.BlockSpec((None, nrow, V_HEAD_DIM), seq3), new_spec, new_spec]
                + [page_spec(r) for r in range(pg)] + [page_spec(r) for r in range(pg)])
    grid_spec = pltpu.PrefetchScalarGridSpec(
        num_scalar_prefetch=1,
        grid=(m // tm, d_ff // tf),
        in_specs=in_specs,
        out_specs=[pl.BlockSpec((tm, D_MODEL), row),
                   pl.BlockSpec((None, N_HEADS * n_new, V_HEAD_DIM), seq3)],
        scratch_shapes=[pltpu.VMEM((tm, D_MODEL), BF16),
                        pltpu.VMEM((nrow, 1), F32), pltpu.VMEM((nrow, 1), F32),
                        pltpu.VMEM((nrow, V_HEAD_DIM), F32)],
    )
    return pl.pallas_call(
        functools.partial(_ffn_decode_kernel, lam_init=lam_init, n_new=n_new, nsteps=nsteps,
                          n_pages=n_pages),
        grid_spec=grid_spec,
        out_shape=[jax.ShapeDtypeStruct((m, D_MODEL), F32),
                   jax.ShapeDtypeStruct((batch, N_HEADS * n_new, V_HEAD_DIM), BF16)],
        compiler_params=_cparams(("parallel", "arbitrary")),
        name="ffn_decode",
    )(page_table, x, g, wg, wu, wd, *lams, sub, q_rows, k_new, v_new,
      *([cache_k] * pg), *([cache_v] * pg))


def _ffn(x, g, wg, wu, wd, layer):
    m = x.shape[0]
    tm, tf = _tiles(m)
    d_ff = wd.shape[1]
    assert wg.shape[1:] == (d_ff // tf, D_MODEL, tf)
    assert m % tm == 0 and d_ff % tf == 0
    row = lambda i, f: (i, 0)
    return pl.pallas_call(
        _ffn_kernel,
        grid=(m // tm, d_ff // tf),
        in_specs=[pl.BlockSpec((tm, D_MODEL), row),
                  pl.BlockSpec((1, D_MODEL), lambda i, f: (0, 0)),
                  pl.BlockSpec((None, None, D_MODEL, tf), lambda i, f: (layer, f, 0, 0)),
                  pl.BlockSpec((None, None, D_MODEL, tf), lambda i, f: (layer, f, 0, 0)),
                  pl.BlockSpec((None, tf, D_MODEL), lambda i, f: (layer, f, 0))],
        out_specs=pl.BlockSpec((tm, D_MODEL), row),
        out_shape=jax.ShapeDtypeStruct((m, D_MODEL), F32),
        scratch_shapes=[pltpu.VMEM((tm, D_MODEL), BF16)],
        compiler_params=_cparams(("parallel", "arbitrary")),
        name="ffn",
    )(x, g, wg, wu, wd)


def _rope_tables(pos):
    half = ROPE_DIM // 2
    inv_freq = ROPE_THETA ** (-jnp.arange(0, ROPE_DIM, 2, dtype=F32) / ROPE_DIM)
    ang = pos.astype(F32)[:, None] * inv_freq[None, :]
    cos, sin = jnp.cos(ang), jnp.sin(ang)
    t = pos.shape[0]
    ones = jnp.ones((t, QK_HEAD_DIM - ROPE_DIM), F32)
    zeros = jnp.zeros((t, QK_HEAD_DIM - ROPE_DIM), F32)
    zh = jnp.zeros((t, half), F32)
    rc = jnp.concatenate([cos, cos, ones], axis=1)
    rs1 = jnp.concatenate([-sin, zh, zeros], axis=1)
    rs2 = jnp.concatenate([zh, sin, zeros], axis=1)
    return tuple(jnp.tile(a, (1, MXU_COLS // QK_HEAD_DIM)) for a in (rc, rs1, rs2))


def _block_diag(w):
    h, di, dj = w.shape
    eye = jnp.eye(h, dtype=w.dtype)
    return (eye[:, None, :, None] * w[:, :, None, :]).reshape(h * di, h * dj)


def _hidden_tile_major(w):
    layers, d_in, d_ff = w.shape
    return w.astype(BF16).reshape(layers, d_in, d_ff // FFN_TILE, FFN_TILE).transpose(0, 2, 1, 3)


def _group_mean_matrix():
    r = jnp.arange(MXU_COLS) // QK_HEAD_DIM
    return ((r[:, None] == r[None, :]).astype(F32) / QK_HEAD_DIM).astype(BF16)


def _row(a, l):
    return a[l].reshape(1, -1)


def _lam_init(l):
    return 0.8 - 0.6 * math.exp(-0.3 * l)


def _lambda_vectors(p, l):
    return tuple(_row(p[n], l) for n in ('lambda_q1', 'lambda_k1', 'lambda_q2', 'lambda_k2'))


def _rope_for(start_pos, t, rows):
    tm = _in_proj_tile(rows)
    rope = _rope_tables(start_pos + jnp.arange(t, dtype=jnp.int32))
    if t < tm:
        rope = tuple(jnp.tile(a, (tm // t, 1)) for a in rope)
    return rope


def _project(x, p, l, rope):
    gain = lambda a: jnp.tile(a[l], MXU_COLS // QK_HEAD_DIM).reshape(1, MXU_COLS)
    return _in_proj(x, _row(p['norm_mix'], l), p['w_in_bf'], l, gain(p['q_norm']),
                    gain(p['k_norm']), p['gmat'], *rope)


def _branches(u, gate, xb, conv_buf, lru_buf, h0, p, l, batch, t_real):
    m = u.shape[0]

    def seq3(a, rows):
        a = a.reshape(batch, t_real, a.shape[-1])
        pad = -t_real % rows
        return a if pad == 0 else jnp.pad(a, ((0, 0), (0, pad), (0, 0)))

    conv_w = jnp.pad(p['conv_w'][l], ((0, CONV_PAD - CONV_KERNEL), (0, 0)))
    o_conv, nb_conv = _conv_module(seq3(u, BF16_ROWS), conv_buf, conv_w, _row(p['conv_b'], l),
                                   _row(p['conv_ln_g'], l), _row(p['conv_ln_b'], l), t_real)
    o_lru, nb_lru, h_last = _lru_block(
        seq3(gate, SUBLANES), seq3(xb, SUBLANES), lru_buf, h0, p['lru_conv_w'][l],
        _row(p['lru_conv_b'], l), p['lru_wa_bd'], _row(p['lru_ba'], l), p['lru_wx_bd'],
        _row(p['lru_bx'], l), _row(p['lru_lambda'], l), l, t_real)
    states = (nb_conv[:, CONV_PAD - (CONV_KERNEL - 1):],
              nb_lru[:, LRU_PAD - (LRU_CONV - 1):],
              h_last.reshape(batch, LRU_WIDTH))
    return (o_conv[:, :t_real].reshape(m, CONV_WIDTH), o_lru[:, :t_real].reshape(m, LRU_WIDTH),
            states)


def _decode_queries(q, batch, t):
    q5 = q.reshape(batch, t, N_HEADS, 2, QK_HEAD_DIM)
    zero = jnp.zeros_like(q5[:, :, :, 0])
    q_rows = jnp.stack([jnp.concatenate([q5[:, :, :, 0], zero], axis=-1),
                        jnp.concatenate([zero, q5[:, :, :, 1]], axis=-1)], axis=1)
    return q_rows.transpose(0, 1, 3, 2, 4).reshape(batch, 2 * N_HEADS * t, V_HEAD_DIM)


def kernel(x_prompt, x_sample, cache_k, cache_v, state_conv, state_lru_conv, state_lru_h, page_table,
           norm_mix, w_in, q_norm, k_norm, lambda_q1, lambda_k1, lambda_q2, lambda_k2, subln,
           conv_w, conv_b, conv_ln_g, conv_ln_b, lru_conv_w, lru_conv_b, lru_wa, lru_ba, lru_wx, lru_bx,
           lru_lambda, w_out, norm_ffn, w_ffn_gate, w_ffn_up, w_ffn_down):
    p = dict(norm_mix=norm_mix, q_norm=q_norm, k_norm=k_norm,
             lambda_q1=lambda_q1, lambda_k1=lambda_k1, lambda_q2=lambda_q2, lambda_k2=lambda_k2,
             subln=subln, conv_w=conv_w, conv_b=conv_b, conv_ln_g=conv_ln_g, conv_ln_b=conv_ln_b,
             lru_conv_w=lru_conv_w, lru_conv_b=lru_conv_b, lru_ba=lru_ba, lru_bx=lru_bx,
             lru_lambda=lru_lambda, norm_ffn=norm_ffn)
    p['w_in_bf'] = w_in.astype(BF16)
    p['w_out_bf'] = w_out.astype(BF16)
    p['w_gate_bf'] = _hidden_tile_major(w_ffn_gate)
    p['w_up_bf'] = _hidden_tile_major(w_ffn_up)
    p['w_down_bf'] = w_ffn_down.astype(BF16)
    p['lru_wa_bd'] = jax.vmap(_block_diag)(lru_wa).astype(BF16)
    p['lru_wx_bd'] = jax.vmap(_block_diag)(lru_wx).astype(BF16)
    p['gmat'] = _group_mean_matrix()

    depth = w_in.shape[0]
    ffn_w = (p['w_gate_bf'], p['w_up_bf'], p['w_down_bf'])

    bp, tp, _ = x_prompt.shape
    bs, ts, _ = x_sample.shape
    mp, ms = bp * tp, bs * ts
    past_len = page_table.shape[1] * PAGE_SIZE
    rope_p = _rope_for(0, tp, mp)
    rope_s = _rope_for(past_len, ts, ms)
    xp = x_prompt.reshape(mp, D_MODEL)
    xs = x_sample.reshape(ms, D_MODEL)
    outs_p, outs_s = [], []
    for l in range(depth):
        lam_init = _lam_init(l)
        lams = _lambda_vectors(p, l)
        sub = _row(p['subln'], l)

        q, k, kb, v, vb, u, gate, xb = _project(xp, p, l, rope_p)
        o_attn = _attn_prompt(lams, sub, q, kb, vb, lam_init, bp, tp)
        o_conv, o_lru, st = _branches(
            u, gate, xb, jnp.zeros((bp, CONV_PAD, CONV_WIDTH), F32),
            jnp.zeros((bp, LRU_PAD, LRU_WIDTH), F32), jnp.zeros((bp, 1, LRU_WIDTH), F32),
            p, l, bp, tp)
        xp = _out_proj(xp, o_attn, o_conv, o_lru, p['w_out_bf'], l)
        outs_p.append((k.reshape(bp, tp, N_HEADS, V_HEAD_DIM),
                       v.reshape(bp, tp, N_HEADS, V_HEAD_DIM)) + st)

        q, k, kb, v, vb, u, gate, xb = _project(xs, p, l, rope_s)
        k_new = k.reshape(bs, ts, N_HEADS, V_HEAD_DIM)
        v_new = v.reshape(bs, ts, N_HEADS, V_HEAD_DIM)
        xp, o = _ffn_decode(xp, _row(p['norm_ffn'], l), *ffn_w, l, page_table, lams, sub,
                            _decode_queries(q, bs, ts), k_new, v_new, cache_k, cache_v, lam_init)
        o_attn = o.reshape(bs, N_HEADS, ts, V_HEAD_DIM).transpose(0, 2, 1, 3).reshape(ms, ATTN_WIDTH)
        o_conv, o_lru, st = _branches(
            u, gate, xb,
            jnp.pad(state_conv[l], ((0, 0), (CONV_PAD - (CONV_KERNEL - 1), 0), (0, 0))),
            jnp.pad(state_lru_conv[l], ((0, 0), (LRU_PAD - (LRU_CONV - 1), 0), (0, 0))),
            state_lru_h[l].reshape(bs, 1, LRU_WIDTH), p, l, bs, ts)
        xs = _out_proj(xs, o_attn, o_conv, o_lru, p['w_out_bf'], l)
        xs = _ffn(xs, _row(p['norm_ffn'], l), *ffn_w, l)
        outs_s.append((k_new, v_new) + st)

    stack = lambda outs: tuple(jnp.stack([o[i] for o in outs]) for i in range(5))
    k_p, v_p, cb_p, lcb_p, h_p = stack(outs_p)
    k_s, v_s, cb_s, lcb_s, h_s = stack(outs_s)
    return (xp.reshape(bp, tp, D_MODEL), xs.reshape(bs, ts, D_MODEL),
            k_p, v_p, cb_p, lcb_p, h_p, k_s, v_s, cb_s, lcb_s, h_s)
```

```python
import functools
import math

import jax
import jax.numpy as jnp
from jax import lax
from jax.experimental import pallas as pl
from jax.experimental.pallas import tpu as pltpu

F32 = jnp.float32
BF16 = jnp.bfloat16

D_MODEL = 2048
N_HEADS = 8
V_HEAD_DIM = 128
QK_HEAD_DIM = 64
ROPE_DIM = 16
ROPE_THETA = 500000.0
ATTN_SCALE = 1.0 / math.sqrt(QK_HEAD_DIM)
LOG2_E = 1.4426950408889634
Q_SCALE = ATTN_SCALE * LOG2_E
ATTN_WIDTH = N_HEADS * V_HEAD_DIM
CONV_WIDTH = 512
LRU_WIDTH = 512
CONV_KERNEL = 31
LRU_CONV = 4
LRU_C = 8.0
EPS = 1e-6
NEG_INF = -1e30
PAGE_SIZE = 128
PAGES_PER_STEP = 6
GROUP_COLS = 1024
MXU_COLS = 256
BF16_ROWS = 16
VMEM_LIMIT = 56 * 1024 * 1024


FFN_TILE = 512


def _tiles(m):
    return min(512, m), FFN_TILE


def _in_proj_tile(m):
    return min(256, m)


def _cparams(sem):
    return pltpu.CompilerParams(dimension_semantics=sem, vmem_limit_bytes=VMEM_LIMIT)


def _sigmoid(x):
    return 1.0 / (1.0 + jnp.exp(-x))


def _silu(x):
    return x * _sigmoid(x)


def _gelu_tanh(x):
    c = math.sqrt(2.0 / math.pi)
    return x * (0.5 * (1.0 + jnp.tanh(c * (x + 0.044715 * (x * x * x)))))


def _log1p(z):
    w = 1.0 + z
    small = w == 1.0
    return jnp.where(small, z, jnp.log(w) * z / jnp.where(small, 1.0, w - 1.0))


def _expm1_nonpos(x):
    u = jnp.exp(x)
    direct = (u == 1.0) | (x < -20.0)
    ratio = (u - 1.0) * x / jnp.where(direct, 1.0, jnp.log(u))
    return jnp.where(u == 1.0, x, jnp.where(x < -20.0, u - 1.0, ratio))


def _lambda_full(lq1, lk1, lq2, lk2, lam_init):
    s1 = jnp.sum(lq1[...] * lk1[...], axis=-1, keepdims=True)
    s2 = jnp.sum(lq2[...] * lk2[...], axis=-1, keepdims=True)
    return jnp.exp(s1) - jnp.exp(s2) + lam_init


def _qk_chunk(y, gain, gmat, rc, rs1, rs2, scale):
    ms = jnp.dot((y * y).astype(BF16), gmat, preferred_element_type=F32)
    yn = y * lax.rsqrt(ms + EPS) * gain
    rot = yn * rc + pltpu.roll(yn, MXU_COLS - ROPE_DIM // 2, 1) * rs1 \
        + pltpu.roll(yn, ROPE_DIM // 2, 1) * rs2
    return rot * scale if scale != 1.0 else rot


def _in_proj_kernel(x_ref, g_ref, w_ref, qn_ref, kn_ref, gmat_ref, rc_ref, rs1_ref, rs2_ref,
                    q_ref, k_ref, kb_ref, v_ref, vb_ref, u_ref, gate_ref, xb_ref, xn_scr):
    x = x_ref[...]
    ms = jnp.mean(x * x, axis=-1, keepdims=True)
    xn_scr[...] = (x * lax.rsqrt(ms + EPS) * g_ref[...]).astype(BF16)

    def cols(c):
        return slice(c * MXU_COLS, (c + 1) * MXU_COLS)

    def proj(group):
        w = w_ref[:, group * GROUP_COLS:(group + 1) * GROUP_COLS]
        return jnp.dot(xn_scr[...], w, preferred_element_type=F32)

    y = proj(0)
    for c in range(GROUP_COLS // MXU_COLS):
        r = _qk_chunk(y[:, cols(c)], qn_ref[...], gmat_ref[...], rc_ref[...], rs1_ref[...],
                      rs2_ref[...], Q_SCALE)
        q_ref[:, cols(c)] = r.astype(BF16)

    y = proj(1)
    for c in range(GROUP_COLS // MXU_COLS):
        r = _qk_chunk(y[:, cols(c)], kn_ref[...], gmat_ref[...], rc_ref[...], rs1_ref[...],
                      rs2_ref[...], 1.0)
        k_ref[:, cols(c)] = r
        kb_ref[:, cols(c)] = r.astype(BF16)

    y = proj(2)
    v_ref[...] = y
    vb_ref[...] = y.astype(BF16)

    y = proj(3)
    u_ref[...] = y[:, :CONV_WIDTH] * _sigmoid(y[:, CONV_WIDTH:])

    y = proj(4)
    gate_ref[...] = _gelu_tanh(y[:, :LRU_WIDTH])
    xb_ref[...] = y[:, LRU_WIDTH:]


def _in_proj(x, g, w, layer, qn, kn, gmat, rc, rs1, rs2):
    m = x.shape[0]
    tm = _in_proj_tile(m)
    assert m % tm == 0 and rc.shape[0] % tm == 0
    nt = m // tm
    nrope = rc.shape[0] // tm
    row = lambda i: (i, 0)
    const = lambda i: (0, 0)
    rope = lambda i: (i % nrope, 0)
    out_shape = [
        jax.ShapeDtypeStruct((m, ATTN_WIDTH), BF16),
        jax.ShapeDtypeStruct((m, ATTN_WIDTH), F32),
        jax.ShapeDtypeStruct((m, ATTN_WIDTH), BF16),
        jax.ShapeDtypeStruct((m, ATTN_WIDTH), F32),
        jax.ShapeDtypeStruct((m, ATTN_WIDTH), BF16),
        jax.ShapeDtypeStruct((m, CONV_WIDTH), F32),
        jax.ShapeDtypeStruct((m, LRU_WIDTH), F32),
        jax.ShapeDtypeStruct((m, LRU_WIDTH), F32),
    ]
    out_specs = [
        pl.BlockSpec((tm, ATTN_WIDTH), row), pl.BlockSpec((tm, ATTN_WIDTH), row),
        pl.BlockSpec((tm, ATTN_WIDTH), row), pl.BlockSpec((tm, ATTN_WIDTH), row),
        pl.BlockSpec((tm, ATTN_WIDTH), row), pl.BlockSpec((tm, CONV_WIDTH), row),
        pl.BlockSpec((tm, LRU_WIDTH), row), pl.BlockSpec((tm, LRU_WIDTH), row),
    ]
    in_specs = [
        pl.BlockSpec((tm, D_MODEL), row),
        pl.BlockSpec((1, D_MODEL), const),
        pl.BlockSpec((None, D_MODEL, 5 * GROUP_COLS), lambda i: (layer, 0, 0),
                     pipeline_mode=pl.Buffered(1)),
        pl.BlockSpec((1, MXU_COLS), const), pl.BlockSpec((1, MXU_COLS), const),
        pl.BlockSpec((MXU_COLS, MXU_COLS), const),
        pl.BlockSpec((tm, MXU_COLS), rope), pl.BlockSpec((tm, MXU_COLS), rope),
        pl.BlockSpec((tm, MXU_COLS), rope),
    ]
    return pl.pallas_call(
        _in_proj_kernel,
        grid=(nt,),
        in_specs=in_specs,
        out_specs=out_specs,
        out_shape=out_shape,
        scratch_shapes=[pltpu.VMEM((tm, D_MODEL), BF16)],
        compiler_params=_cparams(("parallel",)),
        name="in_proj",
    )(x, g, w, qn, kn, gmat, rc, rs1, rs2)


HEADS_PER_STEP = 2


def _attn_prompt_kernel(lq1, lk1, lq2, lk2, sub_ref, q_ref, k_ref, v_ref, o_ref,
                        qt_scr, vt_scr, m_scr, acc_scr, sa_scr, sb_scr, *, lam_init, seq, tq):
    lam = _lambda_full(lq1, lk1, lq2, lk2, lam_init)
    heads = range(HEADS_PER_STEP)
    hcols = lambda h: slice(h * V_HEAD_DIM, (h + 1) * V_HEAD_DIM)
    for c in range(seq // tq):
        sl = slice(c * tq, (c + 1) * tq)
        qt_scr[:, sl] = q_ref[sl, :].astype(F32).T.astype(BF16)
        vt = v_ref[sl, :].astype(F32).T.astype(BF16)
        for h in heads:
            vt_scr[h, 0:V_HEAD_DIM, sl] = vt[hcols(h)]
    first_row = lax.broadcasted_iota(jnp.int32, (BF16_ROWS, seq), 0) == 0
    for h in heads:
        vt_scr[h, V_HEAD_DIM:V_HEAD_DIM + BF16_ROWS, :] = \
            jnp.where(first_row, 1.0, 0.0).astype(BF16)

    key = lax.broadcasted_iota(jnp.int32, (tq, 2 * tq), 0)
    col = lax.broadcasted_iota(jnp.int32, (tq, 2 * tq), 1)
    causal = key <= jnp.where(col >= tq, col - tq, col)
    zeros = jnp.zeros((QK_HEAD_DIM, tq), BF16)

    bufs = (sa_scr, sb_scr)

    def produce(qs, buf, start):
        for h in heads:
            buf[h] = jnp.dot(k_ref[pl.ds(start, tq), hcols(h)], qs[h],
                             preferred_element_type=F32)

    def consume(buf, start, masked):
        s = [buf[h] for h in heads]
        if masked:
            s = [jnp.where(causal, sh, NEG_INF) for sh in s]
        m_new = [jnp.maximum(m_scr[h], jnp.max(s[h], axis=0, keepdims=True)) for h in heads]
        for h in heads:
            alpha = jnp.exp2(m_scr[h] - m_new[h])
            p = jnp.exp2(s[h] - m_new[h]).astype(BF16)
            pv = jnp.dot(vt_scr[h, :, pl.ds(start, tq)], p, preferred_element_type=F32)
            acc_scr[h] = alpha * acc_scr[h] + pv
            m_scr[h] = m_new[h]

    for qi in range(seq // tq):
        qs = []
        for h in heads:
            qt = qt_scr[hcols(h), qi * tq:(qi + 1) * tq]
            qs.append(jnp.concatenate(
                [jnp.concatenate([qt[:QK_HEAD_DIM], zeros], axis=0),
                 jnp.concatenate([zeros, qt[QK_HEAD_DIM:]], axis=0)], axis=1))
        m_scr[...] = jnp.full(m_scr.shape, NEG_INF, F32)
        acc_scr[...] = jnp.zeros(acc_scr.shape, F32)

        produce(qs, bufs[0], 0)
        npairs = qi // 2

        def pair(i, carry):
            base = pl.multiple_of(i * (2 * tq), 2 * tq)
            mid = pl.multiple_of(base + tq, tq)
            produce(qs, bufs[1], mid)
            consume(bufs[0], base, False)
            produce(qs, bufs[0], pl.multiple_of(base + 2 * tq, 2 * tq))
            consume(bufs[1], mid, False)
            return carry

        if npairs > 0:
            lax.fori_loop(0, npairs, pair, 0)
        last = 0
        if qi % 2:
            produce(qs, bufs[1], qi * tq)
            consume(bufs[0], (qi - 1) * tq, False)
            last = 1
        consume(bufs[last], qi * tq, True)

        for h in heads:
            acc = acc_scr[h]
            on = acc[0:V_HEAD_DIM] / acc[V_HEAD_DIM:V_HEAD_DIM + 1]
            d = on[:, :tq] - lam * on[:, tq:]
            ms = jnp.mean(d * d, axis=0, keepdims=True)
            y = d * lax.rsqrt(ms + EPS) * sub_ref[...] * (1.0 - lam_init)
            o_ref[qi * tq:(qi + 1) * tq, hcols(h)] = y.T.astype(o_ref.dtype)


def _attn_prompt(lams, sub, q, kb, vb, lam_init, batch, seq, tq=256):
    hp = HEADS_PER_STEP
    assert seq % tq == 0 and N_HEADS % hp == 0
    vec = pl.BlockSpec((1, QK_HEAD_DIM), lambda b, h: (0, 0))
    blk = pl.BlockSpec((seq, hp * V_HEAD_DIM), lambda b, h: (b, h))
    sub_cols = jnp.broadcast_to(sub.reshape(V_HEAD_DIM, 1), (V_HEAD_DIM, tq))
    return pl.pallas_call(
        functools.partial(_attn_prompt_kernel, lam_init=lam_init, seq=seq, tq=tq),
        grid=(batch, N_HEADS // hp),
        in_specs=[vec, vec, vec, vec, pl.BlockSpec((V_HEAD_DIM, tq), lambda b, h: (0, 0)),
                  blk, blk, blk],
        out_specs=blk,
        out_shape=jax.ShapeDtypeStruct((batch * seq, ATTN_WIDTH), BF16),
        scratch_shapes=[pltpu.VMEM((hp * V_HEAD_DIM, seq), BF16),
                        pltpu.VMEM((hp, V_HEAD_DIM + BF16_ROWS, seq), BF16),
                        pltpu.VMEM((hp, 1, 2 * tq), F32),
                        pltpu.VMEM((hp, V_HEAD_DIM + BF16_ROWS, 2 * tq), F32),
                        pltpu.VMEM((hp, tq, 2 * tq), F32),
                        pltpu.VMEM((hp, tq, 2 * tq), F32)],
        compiler_params=_cparams(("parallel", "parallel")),
        name="attn_prompt",
    )(*lams, sub_cols, q, kb, vb)


def _decode_step(step, nsteps, lams, sub_ref, q_ref, kn_ref, vn_ref, k_refs, v_refs, o_ref,
                 m_scr, l_scr, acc_scr, *, lam_init, n_new, n_pages):
    g = PAGES_PER_STEP
    lq1, lk1, lq2, lk2 = lams
    nrow = 2 * N_HEADS * n_new
    q = q_ref[...]
    nt_dims = (((1,), (1,)), ((), ()))

    def head_of_row(shape):
        r = lax.broadcasted_iota(jnp.int32, shape, 0)
        return (r % (N_HEADS * n_new)) // n_new, r % n_new

    def update(s, vmat):
        m = m_scr[...]
        m_new = jnp.maximum(m, jnp.max(s, axis=-1, keepdims=True))
        alpha = jnp.exp2(m - m_new)
        p = jnp.exp2(s - m_new)
        l_scr[...] = alpha * l_scr[...] + jnp.sum(p, axis=-1, keepdims=True)
        acc_scr[...] = alpha * acc_scr[...] + jnp.dot(p.astype(BF16), vmat,
                                                      preferred_element_type=F32)
        m_scr[...] = m_new

    @pl.when(step == 0)
    def _():
        m_scr[...] = jnp.full(m_scr.shape, NEG_INF, F32)
        l_scr[...] = jnp.zeros(l_scr.shape, F32)
        acc_scr[...] = jnp.zeros(acc_scr.shape, F32)
        kn = kn_ref[...].reshape(n_new * N_HEADS, V_HEAD_DIM).astype(BF16)
        vn = vn_ref[...].reshape(n_new * N_HEADS, V_HEAD_DIM).astype(BF16)
        s = lax.dot_general(q, kn, nt_dims, preferred_element_type=F32)
        shape = s.shape
        hrow, trow = head_of_row(shape)
        c = lax.broadcasted_iota(jnp.int32, shape, 1)
        ok = (hrow == c % N_HEADS) & (c // N_HEADS <= trow)
        update(jnp.where(ok, s, NEG_INF), vn)

    ncol = PAGE_SIZE * N_HEADS
    hrow, _ = head_of_row((nrow, ncol))
    ok = hrow == lax.broadcasted_iota(jnp.int32, (nrow, ncol), 1) % N_HEADS
    s_list = []
    m_new = m_scr[...]
    for p in range(g):
        kp = k_refs[p][...].reshape(ncol, V_HEAD_DIM).astype(BF16)
        s = lax.dot_general(q, kp, nt_dims, preferred_element_type=F32)
        s = jnp.where(ok, s, NEG_INF)
        if n_pages % g:
            s = jnp.where(step * g + p < n_pages, s, NEG_INF)
        s_list.append(s)
        m_new = jnp.maximum(m_new, jnp.max(s, axis=-1, keepdims=True))
    alpha = jnp.exp2(m_scr[...] - m_new)
    l = alpha * l_scr[...]
    acc = alpha * acc_scr[...]
    for p in range(g):
        pr = jnp.exp2(s_list[p] - m_new)
        l = l + jnp.sum(pr, axis=-1, keepdims=True)
        vp = v_refs[p][...].reshape(ncol, V_HEAD_DIM).astype(BF16)
        acc = acc + jnp.dot(pr.astype(BF16), vp, preferred_element_type=F32)
    m_scr[...] = m_new
    l_scr[...] = l
    acc_scr[...] = acc

    @pl.when(step == nsteps - 1)
    def _():
        lam = _lambda_full(lq1, lk1, lq2, lk2, lam_init)
        on = acc_scr[...] / l_scr[...]
        half = N_HEADS * n_new
        o = on[:half] - lam * on[half:]
        ms = jnp.mean(o * o, axis=-1, keepdims=True)
        o = o * lax.rsqrt(ms + EPS) * sub_ref[...] * (1.0 - lam_init)
        o_ref[...] = o.astype(o_ref.dtype)


CONV_PAD = 32
CONV_CHUNK = 32
CONV_UNROLL = 4
SUBLANES = 8


def _shift_matrix(nwin):
    span = nwin - SUBLANES
    r = jnp.arange((SUBLANES - 1) * span)
    src = r % span + r // span + 1
    one = (src[:, None] == jnp.arange(nwin)[None, :]).astype(BF16)
    return jnp.concatenate([one, one, one], axis=1)


def _conv_kernel(u_ref, buf_ref, w_ref, b_ref, lg_ref, lb_ref, shift_ref, o_ref, nb_ref, pad_scr,
                 *, t_pad, t_real):
    pad_scr[0:CONV_PAD, :] = buf_ref[...]
    pad_scr[CONV_PAD:CONV_PAD + t_pad, :] = u_ref[...]
    off = CONV_PAD - (CONV_KERNEL - 1)
    rc = min(CONV_CHUNK, t_pad)
    nwin = rc + CONV_PAD
    span = nwin - SUBLANES

    def chunk(c0):
        acc = jnp.broadcast_to(b_ref[...], (rc, CONV_WIDTH))
        win = pad_scr[pl.ds(c0, nwin), :]
        hi = win.astype(BF16)
        r1 = win - hi.astype(F32)
        mid = r1.astype(BF16)
        lo = (r1 - mid.astype(F32)).astype(BF16)
        shifted = jnp.dot(shift_ref[...], jnp.concatenate([hi, mid, lo], axis=0),
                          preferred_element_type=F32)
        for s in range(SUBLANES):
            taps = [j for j in range(CONV_KERNEL) if (off + j) % SUBLANES == s]
            sh = win if s == 0 else shifted[(s - 1) * span:s * span]
            for j in taps:
                a0 = off + j - s
                acc = acc + w_ref[j:j + 1, :] * sh[a0:a0 + rc, :]
        mu = jnp.mean(acc, axis=-1, keepdims=True)
        d = acc - mu
        var = jnp.mean(d * d, axis=-1, keepdims=True)
        y = d * lax.rsqrt(var + EPS) * lg_ref[...] + lb_ref[...]
        o_ref[pl.ds(c0, rc), :] = _silu(y).astype(o_ref.dtype)

    if t_pad == rc:
        chunk(0)
    else:
        def body(ci, carry):
            chunk(pl.multiple_of(ci * rc, rc))
            return carry
        lax.fori_loop(0, t_pad // rc, body, 0, unroll=CONV_UNROLL)
    nb_ref[...] = pad_scr[t_real:t_real + CONV_PAD, :]


def _conv_module(u, buf32, w, b, lg, lb, t_real):
    batch, t_pad, _ = u.shape
    vec = pl.BlockSpec((1, CONV_WIDTH), lambda i: (0, 0))
    shift = _shift_matrix(min(CONV_CHUNK, t_pad) + CONV_PAD)
    return pl.pallas_call(
        functools.partial(_conv_kernel, t_pad=t_pad, t_real=t_real),
        grid=(batch,),
        in_specs=[pl.BlockSpec((None, t_pad, CONV_WIDTH), lambda i: (i, 0, 0)),
                  pl.BlockSpec((None, CONV_PAD, CONV_WIDTH), lambda i: (i, 0, 0)),
                  pl.BlockSpec((CONV_PAD, CONV_WIDTH), lambda i: (0, 0)),
                  vec, vec, vec, pl.BlockSpec(shift.shape, lambda i: (0, 0))],
        out_specs=[pl.BlockSpec((None, t_pad, CONV_WIDTH), lambda i: (i, 0, 0)),
                   pl.BlockSpec((None, CONV_PAD, CONV_WIDTH), lambda i: (i, 0, 0))],
        out_shape=[jax.ShapeDtypeStruct((batch, t_pad, CONV_WIDTH), BF16),
                   jax.ShapeDtypeStruct((batch, CONV_PAD, CONV_WIDTH), F32)],
        scratch_shapes=[pltpu.VMEM((CONV_PAD + t_pad, CONV_WIDTH), F32)],
        compiler_params=_cparams(("parallel",)),
        name="conv_module",
    )(u, buf32, w, b, lg, lb, shift)


LRU_PAD = 8
LRU_CHUNK = 256


def _lru_kernel(gate_ref, xb_ref, buf_ref, h0_ref, cw_ref, cb_ref, wa_ref, ba_ref, wx_ref, bx_ref,
                lam_ref, y_ref, nb_ref, hl_ref, pad_scr, a_scr, b_scr, h_scr, *, t_pad, t_real):
    pad_scr[0:LRU_PAD, :] = buf_ref[...]
    pad_scr[LRU_PAD:LRU_PAD + t_pad, :] = xb_ref[...]
    off = LRU_PAD - (LRU_CONV - 1)
    rc = min(LRU_CHUNK, t_pad)
    lam = lam_ref[...]
    sp = jnp.maximum(-lam, 0.0) + _log1p(jnp.exp(-jnp.abs(lam)))

    def gates(c0):
        xc = jnp.broadcast_to(cb_ref[...], (rc, LRU_WIDTH))
        win = pad_scr[pl.ds(c0, rc + LRU_PAD), :]
        for j in range(LRU_CONV):
            xc = xc + cw_ref[j:j + 1, :] * win[off + j:off + j + rc, :]
        xcb = xc.astype(BF16)
        r = _sigmoid(jnp.dot(xcb, wa_ref[...], preferred_element_type=F32) + ba_ref[...])
        i = _sigmoid(jnp.dot(xcb, wx_ref[...], preferred_element_type=F32) + bx_ref[...])
        log_a = (-LRU_C) * r * sp
        a_scr[pl.ds(c0, rc), :] = jnp.exp(log_a)
        b_scr[pl.ds(c0, rc), :] = jnp.sqrt(-_expm1_nonpos(2.0 * log_a)) * (i * xc)

    if t_pad == rc:
        gates(0)
    else:
        def gbody(ci, carry):
            gates(pl.multiple_of(ci * rc, rc))
            return carry
        lax.fori_loop(0, t_pad // rc, gbody, 0)

    row = lax.broadcasted_iota(jnp.int32, (8, LRU_WIDTH), 0)

    def sbody(gi, h):
        r0 = pl.multiple_of(gi * 8, 8)
        a = a_scr[pl.ds(r0, 8), :]
        b = b_scr[pl.ds(r0, 8), :]
        for s in (1, 2, 4):
            a_sh = jnp.where(row >= s, pltpu.roll(a, s, 0), 1.0)
            b_sh = jnp.where(row >= s, pltpu.roll(b, s, 0), 0.0)
            b = a * b_sh + b
            a = a * a_sh
        hs = a * h + b
        h_scr[pl.ds(r0, 8), :] = hs
        return hs[7:8, :]

    lax.fori_loop(0, t_pad // 8, sbody, h0_ref[...])
    y_ref[...] = (h_scr[...] * gate_ref[...]).astype(y_ref.dtype)
    nb_ref[...] = pad_scr[t_real:t_real + LRU_PAD, :]
    hl_ref[...] = h_scr[t_real - 1:t_real, :]


def _lru_block(gate, xb, buf8, h0, cw, cb, wa, ba, wx, bx, lam, layer, t_real):
    batch, t_pad, _ = gate.shape
    vec = pl.BlockSpec((1, LRU_WIDTH), lambda i: (0, 0))
    seq = pl.BlockSpec((None, t_pad, LRU_WIDTH), lambda i: (i, 0, 0))
    mat = pl.BlockSpec((None, LRU_WIDTH, LRU_WIDTH), lambda i: (layer, 0, 0))
    return pl.pallas_call(
        functools.partial(_lru_kernel, t_pad=t_pad, t_real=t_real),
        grid=(batch,),
        in_specs=[seq, seq,
                  pl.BlockSpec((None, LRU_PAD, LRU_WIDTH), lambda i: (i, 0, 0)),
                  pl.BlockSpec((None, 1, LRU_WIDTH), lambda i: (i, 0, 0)),
                  pl.BlockSpec((LRU_CONV, LRU_WIDTH), lambda i: (0, 0)),
                  vec, mat, vec, mat, vec, vec],
        out_specs=[seq,
                   pl.BlockSpec((None, LRU_PAD, LRU_WIDTH), lambda i: (i, 0, 0)),
                   pl.BlockSpec((None, 1, LRU_WIDTH), lambda i: (i, 0, 0))],
        out_shape=[jax.ShapeDtypeStruct((batch, t_pad, LRU_WIDTH), BF16),
                   jax.ShapeDtypeStruct((batch, LRU_PAD, LRU_WIDTH), F32),
                   jax.ShapeDtypeStruct((batch, 1, LRU_WIDTH), F32)],
        scratch_shapes=[pltpu.VMEM((LRU_PAD + t_pad, LRU_WIDTH), F32),
                        pltpu.VMEM((t_pad, LRU_WIDTH), F32),
                        pltpu.VMEM((t_pad, LRU_WIDTH), F32),
                        pltpu.VMEM((t_pad, LRU_WIDTH), F32)],
        compiler_params=_cparams(("parallel",)),
        name="lru_block",
    )(gate, xb, buf8, h0, cw, cb, wa, ba, wx, bx, lam)


def _out_proj_kernel(x_ref, oa_ref, oc_ref, ol_ref, w_ref, y_ref):
    acc = x_ref[...]
    acc = acc + jnp.dot(oa_ref[...], w_ref[0:ATTN_WIDTH, :], preferred_element_type=F32)
    acc = acc + jnp.dot(oc_ref[...], w_ref[ATTN_WIDTH:ATTN_WIDTH + CONV_WIDTH, :],
                        preferred_element_type=F32)
    acc = acc + jnp.dot(ol_ref[...], w_ref[ATTN_WIDTH + CONV_WIDTH:, :],
                        preferred_element_type=F32)
    y_ref[...] = acc


def _out_proj(x, oa, oc, ol, w, layer):
    m = x.shape[0]
    tm, _ = _tiles(m)
    assert m % tm == 0
    row = lambda i: (i, 0)
    return pl.pallas_call(
        _out_proj_kernel,
        grid=(m // tm,),
        in_specs=[pl.BlockSpec((tm, D_MODEL), row), pl.BlockSpec((tm, ATTN_WIDTH), row),
                  pl.BlockSpec((tm, CONV_WIDTH), row), pl.BlockSpec((tm, LRU_WIDTH), row),
                  pl.BlockSpec((None, D_MODEL, D_MODEL), lambda i: (layer, 0, 0))],
        out_specs=pl.BlockSpec((tm, D_MODEL), row),
        out_shape=jax.ShapeDtypeStruct((m, D_MODEL), F32),
        compiler_params=_cparams(("parallel",)),
        name="out_proj",
    )(x, oa, oc, ol, w)


def _ffn_step(f, x_ref, g_ref, wg_ref, wu_ref, wd_ref, y_ref, xn_scr):
    @pl.when(f == 0)
    def _():
        x = x_ref[...]
        ms = jnp.mean(x * x, axis=-1, keepdims=True)
        xn_scr[...] = (x * lax.rsqrt(ms + EPS) * g_ref[...]).astype(BF16)
        y_ref[...] = x

    xn = xn_scr[...]
    gate = jnp.dot(xn, wg_ref[...], preferred_element_type=F32)
    up = jnp.dot(xn, wu_ref[...], preferred_element_type=F32)
    hid = (_silu(gate) * up).astype(BF16)
    y_ref[...] += jnp.dot(hid, wd_ref[...], preferred_element_type=F32)


def _ffn_kernel(x_ref, g_ref, wg_ref, wu_ref, wd_ref, y_ref, xn_scr):
    _ffn_step(pl.program_id(1), x_ref, g_ref, wg_ref, wu_ref, wd_ref, y_ref, xn_scr)


def _ffn_decode_kernel(pt_ref, x_ref, g_ref, wg_ref, wu_ref, wd_ref,
                       lq1, lk1, lq2, lk2, sub_ref, q_ref, kn_ref, vn_ref, *rest,
                       lam_init, n_new, nsteps, n_pages):
    del pt_ref
    g = PAGES_PER_STEP
    k_refs, v_refs = rest[:g], rest[g:2 * g]
    y_ref, o_ref, xn_scr, m_scr, l_scr, acc_scr = rest[2 * g:]
    f = pl.program_id(1)
    _ffn_step(f, x_ref, g_ref, wg_ref, wu_ref, wd_ref, y_ref, xn_scr)

    @pl.when(f < nsteps)
    def _():
        _decode_step(f, nsteps, (lq1, lk1, lq2, lk2), sub_ref, q_ref, kn_ref, vn_ref, k_refs, v_refs,
                     o_ref, m_scr, l_scr, acc_scr, lam_init=lam_init, n_new=n_new,
                     n_pages=n_pages)


def _ffn_decode(x, g, wg, wu, wd, layer, page_table, lams, sub, q_rows, k_new, v_new,
                cache_k, cache_v, lam_init):
    m = x.shape[0]
    tm, tf = _tiles(m)
    d_ff = wd.shape[1]
    assert wg.shape[1:] == (d_ff // tf, D_MODEL, tf)
    batch, n_pages = page_table.shape
    n_new = k_new.shape[1]
    pg = PAGES_PER_STEP
    nrow = 2 * N_HEADS * n_new
    nsteps = -(-n_pages // pg)
    assert m % tm == 0 and d_ff % tf == 0
    assert m // tm == batch and d_ff // tf >= nsteps
    row = lambda i, f, pt: (i, 0)
    const = lambda i, f, pt: (0, 0)
    seq3 = lambda i, f, pt: (i, 0, 0)
    new_spec = pl.BlockSpec((None, n_new, N_HEADS, V_HEAD_DIM), lambda i, f, pt: (i, 0, 0, 0))

    def page_spec(r):
        def index(i, f, pt):
            slot = jnp.minimum(jnp.minimum(f, nsteps - 1) * pg + r, n_pages - 1)
            return (layer, pt[i, slot], 0, 0, 0)
        return pl.BlockSpec((None, None, PAGE_SIZE, N_HEADS, V_HEAD_DIM), index)

    vec = pl.BlockSpec((1, QK_HEAD_DIM), const)
    in_specs = ([pl.BlockSpec((tm, D_MODEL), row),
                 pl.BlockSpec((1, D_MODEL), const),
                 pl.BlockSpec((None, None, D_MODEL, tf), lambda i, f, pt: (layer, f, 0, 0)),
                 pl.BlockSpec((None, None, D_MODEL, tf), lambda i, f, pt: (layer, f, 0, 0)),
                 pl.BlockSpec((None, tf, D_MODEL), lambda i, f, pt: (layer, f, 0)),
                 vec, vec, vec, vec, pl.BlockSpec((1, V_HEAD_DIM), const),
                 pl.BlockSpec((None, nrow, V_HEAD_DIM), seq3), new_spec, new_spec]
                + [page_spec(r) for r in range(pg)] + [page_spec(r) for r in range(pg)])
    grid_spec = pltpu.PrefetchScalarGridSpec(
        num_scalar_prefetch=1,
        grid=(m // tm, d_ff // tf),
        in_specs=in_specs,
        out_specs=[pl.BlockSpec((tm, D_MODEL), row),
                   pl.BlockSpec((None, N_HEADS * n_new, V_HEAD_DIM), seq3)],
        scratch_shapes=[pltpu.VMEM((tm, D_MODEL), BF16),
                        pltpu.VMEM((nrow, 1), F32), pltpu.VMEM((nrow, 1), F32),
                        pltpu.VMEM((nrow, V_HEAD_DIM), F32)],
    )
    return pl.pallas_call(
        functools.partial(_ffn_decode_kernel, lam_init=lam_init, n_new=n_new, nsteps=nsteps,
                          n_pages=n_pages),
        grid_spec=grid_spec,
        out_shape=[jax.ShapeDtypeStruct((m, D_MODEL), F32),
                   jax.ShapeDtypeStruct((batch, N_HEADS * n_new, V_HEAD_DIM), BF16)],
        compiler_params=_cparams(("parallel", "arbitrary")),
        name="ffn_decode",
    )(page_table, x, g, wg, wu, wd, *lams, sub, q_rows, k_new, v_new,
      *([cache_k] * pg), *([cache_v] * pg))


def _ffn(x, g, wg, wu, wd, layer):
    m = x.shape[0]
    tm, tf = _tiles(m)
    d_ff = wd.shape[1]
    assert wg.shape[1:] == (d_ff // tf, D_MODEL, tf)
    assert m % tm == 0 and d_ff % tf == 0
    row = lambda i, f: (i, 0)
    return pl.pallas_call(
        _ffn_kernel,
        grid=(m // tm, d_ff // tf),
        in_specs=[pl.BlockSpec((tm, D_MODEL), row),
                  pl.BlockSpec((1, D_MODEL), lambda i, f: (0, 0)),
                  pl.BlockSpec((None, None, D_MODEL, tf), lambda i, f: (layer, f, 0, 0)),
                  pl.BlockSpec((None, None, D_MODEL, tf), lambda i, f: (layer, f, 0, 0)),
                  pl.BlockSpec((None, tf, D_MODEL), lambda i, f: (layer, f, 0))],
        out_specs=pl.BlockSpec((tm, D_MODEL), row),
        out_shape=jax.ShapeDtypeStruct((m, D_MODEL), F32),
        scratch_shapes=[pltpu.VMEM((tm, D_MODEL), BF16)],
        compiler_params=_cparams(("parallel", "arbitrary")),
        name="ffn",
    )(x, g, wg, wu, wd)


def _rope_tables(pos):
    half = ROPE_DIM // 2
    inv_freq = ROPE_THETA ** (-jnp.arange(0, ROPE_DIM, 2, dtype=F32) / ROPE_DIM)
    ang = pos.astype(F32)[:, None] * inv_freq[None, :]
    cos, sin = jnp.cos(ang), jnp.sin(ang)
    t = pos.shape[0]
    ones = jnp.ones((t, QK_HEAD_DIM - ROPE_DIM), F32)
    zeros = jnp.zeros((t, QK_HEAD_DIM - ROPE_DIM), F32)
    zh = jnp.zeros((t, half), F32)
    rc = jnp.concatenate([cos, cos, ones], axis=1)
    rs1 = jnp.concatenate([-sin, zh, zeros], axis=1)
    rs2 = jnp.concatenate([zh, sin, zeros], axis=1)
    return tuple(jnp.tile(a, (1, MXU_COLS // QK_HEAD_DIM)) for a in (rc, rs1, rs2))


def _block_diag(w):
    h, di, dj = w.shape
    eye = jnp.eye(h, dtype=w.dtype)
    return (eye[:, None, :, None] * w[:, :, None, :]).reshape(h * di, h * dj)


def _hidden_tile_major(w):
    layers, d_in, d_ff = w.shape
    tiles = w.reshape(layers, d_in, d_ff // FFN_TILE, FFN_TILE).transpose(0, 2, 1, 3)
    return tiles.astype(BF16)


def _group_mean_matrix():
    r = jnp.arange(MXU_COLS) // QK_HEAD_DIM
    return ((r[:, None] == r[None, :]).astype(F32) / QK_HEAD_DIM).astype(BF16)


def _row(a, l):
    return a[l].reshape(1, -1)


def _lam_init(l):
    return 0.8 - 0.6 * math.exp(-0.3 * l)


def _lambda_vectors(p, l):
    return tuple(_row(p[n], l) for n in ('lambda_q1', 'lambda_k1', 'lambda_q2', 'lambda_k2'))


def _rope_for(start_pos, t, rows):
    tm = _in_proj_tile(rows)
    rope = _rope_tables(start_pos + jnp.arange(t, dtype=jnp.int32))
    if t < tm:
        rope = tuple(jnp.tile(a, (tm // t, 1)) for a in rope)
    return rope


def _project(x, p, l, rope):
    gain = lambda a: jnp.tile(a[l], MXU_COLS // QK_HEAD_DIM).reshape(1, MXU_COLS)
    return _in_proj(x, _row(p['norm_mix'], l), p['w_in_bf'], l, gain(p['q_norm']),
                    gain(p['k_norm']), p['gmat'], *rope)


def _branches(u, gate, xb, conv_buf, lru_buf, h0, p, l, batch, t_real):
    m = u.shape[0]

    def seq3(a, rows):
        a = a.reshape(batch, t_real, a.shape[-1])
        pad = -t_real % rows
        return a if pad == 0 else jnp.pad(a, ((0, 0), (0, pad), (0, 0)))

    conv_w = jnp.pad(p['conv_w'][l], ((0, CONV_PAD - CONV_KERNEL), (0, 0)))
    o_conv, nb_conv = _conv_module(seq3(u, BF16_ROWS), conv_buf, conv_w, _row(p['conv_b'], l),
                                   _row(p['conv_ln_g'], l), _row(p['conv_ln_b'], l), t_real)
    o_lru, nb_lru, h_last = _lru_block(
        seq3(gate, SUBLANES), seq3(xb, SUBLANES), lru_buf, h0, p['lru_conv_w'][l],
        _row(p['lru_conv_b'], l), p['lru_wa_bd'], _row(p['lru_ba'], l), p['lru_wx_bd'],
        _row(p['lru_bx'], l), _row(p['lru_lambda'], l), l, t_real)
    states = (nb_conv[:, CONV_PAD - (CONV_KERNEL - 1):],
              nb_lru[:, LRU_PAD - (LRU_CONV - 1):],
              h_last.reshape(batch, LRU_WIDTH))
    return (o_conv[:, :t_real].reshape(m, CONV_WIDTH), o_lru[:, :t_real].reshape(m, LRU_WIDTH),
            states)


def _decode_queries(q, batch, t):
    q5 = q.reshape(batch, t, N_HEADS, 2, QK_HEAD_DIM)
    zero = jnp.zeros_like(q5[:, :, :, 0])
    q_rows = jnp.stack([jnp.concatenate([q5[:, :, :, 0], zero], axis=-1),
                        jnp.concatenate([zero, q5[:, :, :, 1]], axis=-1)], axis=1)
    return q_rows.transpose(0, 1, 3, 2, 4).reshape(batch, 2 * N_HEADS * t, V_HEAD_DIM)


def kernel(x_prompt, x_sample, cache_k, cache_v, state_conv, state_lru_conv, state_lru_h, page_table,
           norm_mix, w_in, q_norm, k_norm, lambda_q1, lambda_k1, lambda_q2, lambda_k2, subln,
           conv_w, conv_b, conv_ln_g, conv_ln_b, lru_conv_w, lru_conv_b, lru_wa, lru_ba, lru_wx, lru_bx,
           lru_lambda, w_out, norm_ffn, w_ffn_gate, w_ffn_up, w_ffn_down):
    p = dict(norm_mix=norm_mix, q_norm=q_norm, k_norm=k_norm,
             lambda_q1=lambda_q1, lambda_k1=lambda_k1, lambda_q2=lambda_q2, lambda_k2=lambda_k2,
             subln=subln, conv_w=conv_w, conv_b=conv_b, conv_ln_g=conv_ln_g, conv_ln_b=conv_ln_b,
             lru_conv_w=lru_conv_w, lru_conv_b=lru_conv_b, lru_ba=lru_ba, lru_bx=lru_bx,
             lru_lambda=lru_lambda, norm_ffn=norm_ffn)
    p['w_in_bf'] = w_in.astype(BF16)
    p['w_out_bf'] = w_out.astype(BF16)
    p['w_gate_bf'] = _hidden_tile_major(w_ffn_gate)
    p['w_up_bf'] = _hidden_tile_major(w_ffn_up)
    p['w_down_bf'] = w_ffn_down.astype(BF16)
    p['lru_wa_bd'] = jax.vmap(_block_diag)(lru_wa).astype(BF16)
    p['lru_wx_bd'] = jax.vmap(_block_diag)(lru_wx).astype(BF16)
    p['gmat'] = _group_mean_matrix()

    depth = w_in.shape[0]
    ffn_w = (p['w_gate_bf'], p['w_up_bf'], p['w_down_bf'])

    bp, tp, _ = x_prompt.shape
    bs, ts, _ = x_sample.shape
    mp, ms = bp * tp, bs * ts
    past_len = page_table.shape[1] * PAGE_SIZE
    rope_p = _rope_for(0, tp, mp)
    rope_s = _rope_for(past_len, ts, ms)
    xp = x_prompt.reshape(mp, D_MODEL)
    xs = x_sample.reshape(ms, D_MODEL)
    outs_p, outs_s = [], []
    for l in range(depth):
        lam_init = _lam_init(l)
        lams = _lambda_vectors(p, l)
        sub = _row(p['subln'], l)

        q, k, kb, v, vb, u, gate, xb = _project(xp, p, l, rope_p)
        o_attn = _attn_prompt(lams, sub, q, kb, vb, lam_init, bp, tp)
        o_conv, o_lru, st = _branches(
            u, gate, xb, jnp.zeros((bp, CONV_PAD, CONV_WIDTH), F32),
            jnp.zeros((bp, LRU_PAD, LRU_WIDTH), F32), jnp.zeros((bp, 1, LRU_WIDTH), F32),
            p, l, bp, tp)
        xp = _out_proj(xp, o_attn, o_conv, o_lru, p['w_out_bf'], l)
        outs_p.append((k.reshape(bp, tp, N_HEADS, V_HEAD_DIM),
                       v.reshape(bp, tp, N_HEADS, V_HEAD_DIM)) + st)

        q, k, kb, v, vb, u, gate, xb = _project(xs, p, l, rope_s)
        k_new = k.reshape(bs, ts, N_HEADS, V_HEAD_DIM)
        v_new = v.reshape(bs, ts, N_HEADS, V_HEAD_DIM)
        xp, o = _ffn_decode(xp, _row(p['norm_ffn'], l), *ffn_w, l, page_table, lams, sub,
                            _decode_queries(q, bs, ts), k_new, v_new, cache_k, cache_v, lam_init)
        o_attn = o.reshape(bs, N_HEADS, ts, V_HEAD_DIM).transpose(0, 2, 1, 3).reshape(ms, ATTN_WIDTH)
        o_conv, o_lru, st = _branches(
            u, gate, xb,
            jnp.pad(state_conv[l], ((0, 0), (CONV_PAD - (CONV_KERNEL - 1), 0), (0, 0))),
            jnp.pad(state_lru_conv[l], ((0, 0), (LRU_PAD - (LRU_CONV - 1), 0), (0, 0))),
            state_lru_h[l].reshape(bs, 1, LRU_WIDTH), p, l, bs, ts)
        xs = _out_proj(xs, o_attn, o_conv, o_lru, p['w_out_bf'], l)
        xs = _ffn(xs, _row(p['norm_ffn'], l), *ffn_w, l)
        outs_s.append((k_new, v_new) + st)

    stack = lambda outs: tuple(jnp.stack([o[i] for o in outs]) for i in range(5))
    k_p, v_p, cb_p, lcb_p, h_p = stack(outs_p)
    k_s, v_s, cb_s, lcb_s, h_s = stack(outs_s)
    return (xp.reshape(bp, tp, D_MODEL), xs.reshape(bs, ts, D_MODEL),
            k_p, v_p, cb_p, lcb_p, h_p, k_s, v_s, cb_s, lcb_s, h_s)
```

```python
import functools
import math

import jax
import jax.numpy as jnp
from jax import lax
from jax.experimental import pallas as pl
from jax.experimental.pallas import tpu as pltpu

F32 = jnp.float32
BF16 = jnp.bfloat16

D_MODEL = 2048
N_HEADS = 8
V_HEAD_DIM = 128
QK_HEAD_DIM = 64
ROPE_DIM = 16
ROPE_THETA = 500000.0
ATTN_SCALE = 1.0 / math.sqrt(QK_HEAD_DIM)
LOG2_E = 1.4426950408889634
Q_SCALE = ATTN_SCALE * LOG2_E
ATTN_WIDTH = N_HEADS * V_HEAD_DIM
CONV_WIDTH = 512
LRU_WIDTH = 512
CONV_KERNEL = 31
LRU_CONV = 4
LRU_C = 8.0
EPS = 1e-6
NEG_INF = -1e30
PAGE_SIZE = 128
PAGES_PER_STEP = 6
GROUP_COLS = 1024
MXU_COLS = 256
BF16_ROWS = 16
VMEM_LIMIT = 56 * 1024 * 1024


FFN_TILE = 512


def _tiles(m):
    return min(512, m), FFN_TILE


def _in_proj_tile(m):
    return min(256, m)


def _cparams(sem):
    return pltpu.CompilerParams(dimension_semantics=sem, vmem_limit_bytes=VMEM_LIMIT)


def _sigmoid(x):
    return 1.0 / (1.0 + jnp.exp(-x))


def _silu(x):
    return x * _sigmoid(x)


def _gelu_tanh(x):
    c = math.sqrt(2.0 / math.pi)
    return x * (0.5 * (1.0 + jnp.tanh(c * (x + 0.044715 * (x * x * x)))))


def _log1p(z):
    w = 1.0 + z
    small = w == 1.0
    return jnp.where(small, z, jnp.log(w) * z / jnp.where(small, 1.0, w - 1.0))


def _expm1_nonpos(x):
    u = jnp.exp(x)
    direct = (u == 1.0) | (x < -20.0)
    ratio = (u - 1.0) * x / jnp.where(direct, 1.0, jnp.log(u))
    return jnp.where(u == 1.0, x, jnp.where(x < -20.0, u - 1.0, ratio))


def _lambda_full(lq1, lk1, lq2, lk2, lam_init):
    s1 = jnp.sum(lq1[...] * lk1[...], axis=-1, keepdims=True)
    s2 = jnp.sum(lq2[...] * lk2[...], axis=-1, keepdims=True)
    return jnp.exp(s1) - jnp.exp(s2) + lam_init


def _qk_chunk(y, gain, gmat, rc, rs1, rs2, scale):
    ms = jnp.dot((y * y).astype(BF16), gmat, preferred_element_type=F32)
    yn = y * lax.rsqrt(ms + EPS) * gain
    rot = yn * rc + pltpu.roll(yn, MXU_COLS - ROPE_DIM // 2, 1) * rs1 \
        + pltpu.roll(yn, ROPE_DIM // 2, 1) * rs2
    return rot * scale if scale != 1.0 else rot


def _in_proj_kernel(x_ref, g_ref, w_ref, qn_ref, kn_ref, gmat_ref, rc_ref, rs1_ref, rs2_ref,
                    q_ref, k_ref, kb_ref, v_ref, vb_ref, u_ref, gate_ref, xb_ref, xn_scr):
    x = x_ref[...]
    ms = jnp.mean(x * x, axis=-1, keepdims=True)
    xn_scr[...] = (x * lax.rsqrt(ms + EPS) * g_ref[...]).astype(BF16)

    def cols(c):
        return slice(c * MXU_COLS, (c + 1) * MXU_COLS)

    def proj(group):
        w = w_ref[:, group * GROUP_COLS:(group + 1) * GROUP_COLS]
        return jnp.dot(xn_scr[...], w, preferred_element_type=F32)

    y = proj(0)
    for c in range(GROUP_COLS // MXU_COLS):
        r = _qk_chunk(y[:, cols(c)], qn_ref[...], gmat_ref[...], rc_ref[...], rs1_ref[...],
                      rs2_ref[...], Q_SCALE)
        q_ref[:, cols(c)] = r.astype(BF16)

    y = proj(1)
    for c in range(GROUP_COLS // MXU_COLS):
        r = _qk_chunk(y[:, cols(c)], kn_ref[...], gmat_ref[...], rc_ref[...], rs1_ref[...],
                      rs2_ref[...], 1.0)
        k_ref[:, cols(c)] = r
        kb_ref[:, cols(c)] = r.astype(BF16)

    y = proj(2)
    v_ref[...] = y
    vb_ref[...] = y.astype(BF16)

    y = proj(3)
    u_ref[...] = y[:, :CONV_WIDTH] * _sigmoid(y[:, CONV_WIDTH:])

    y = proj(4)
    gate_ref[...] = _gelu_tanh(y[:, :LRU_WIDTH])
    xb_ref[...] = y[:, LRU_WIDTH:]


def _in_proj(x, g, w, layer, qn, kn, gmat, rc, rs1, rs2):
    m = x.shape[0]
    tm = _in_proj_tile(m)
    assert m % tm == 0 and rc.shape[0] % tm == 0
    nt = m // tm
    nrope = rc.shape[0] // tm
    row = lambda i: (i, 0)
    const = lambda i: (0, 0)
    rope = lambda i: (i % nrope, 0)
    out_shape = [
        jax.ShapeDtypeStruct((m, ATTN_WIDTH), BF16),
        jax.ShapeDtypeStruct((m, ATTN_WIDTH), F32),
        jax.ShapeDtypeStruct((m, ATTN_WIDTH), BF16),
        jax.ShapeDtypeStruct((m, ATTN_WIDTH), F32),
        jax.ShapeDtypeStruct((m, ATTN_WIDTH), BF16),
        jax.ShapeDtypeStruct((m, CONV_WIDTH), F32),
        jax.ShapeDtypeStruct((m, LRU_WIDTH), F32),
        jax.ShapeDtypeStruct((m, LRU_WIDTH), F32),
    ]
    out_specs = [
        pl.BlockSpec((tm, ATTN_WIDTH), row), pl.BlockSpec((tm, ATTN_WIDTH), row),
        pl.BlockSpec((tm, ATTN_WIDTH), row), pl.BlockSpec((tm, ATTN_WIDTH), row),
        pl.BlockSpec((tm, ATTN_WIDTH), row), pl.BlockSpec((tm, CONV_WIDTH), row),
        pl.BlockSpec((tm, LRU_WIDTH), row), pl.BlockSpec((tm, LRU_WIDTH), row),
    ]
    in_specs = [
        pl.BlockSpec((tm, D_MODEL), row),
        pl.BlockSpec((1, D_MODEL), const),
        pl.BlockSpec((None, D_MODEL, 5 * GROUP_COLS), lambda i: (layer, 0, 0),
                     pipeline_mode=pl.Buffered(1)),
        pl.BlockSpec((1, MXU_COLS), const), pl.BlockSpec((1, MXU_COLS), const),
        pl.BlockSpec((MXU_COLS, MXU_COLS), const),
        pl.BlockSpec((tm, MXU_COLS), rope), pl.BlockSpec((tm, MXU_COLS), rope),
        pl.BlockSpec((tm, MXU_COLS), rope),
    ]
    return pl.pallas_call(
        _in_proj_kernel,
        grid=(nt,),
        in_specs=in_specs,
        out_specs=out_specs,
        out_shape=out_shape,
        scratch_shapes=[pltpu.VMEM((tm, D_MODEL), BF16)],
        compiler_params=_cparams(("parallel",)),
        name="in_proj",
    )(x, g, w, qn, kn, gmat, rc, rs1, rs2)


HEADS_PER_STEP = 4


def _attn_prompt_kernel(lq1, lk1, lq2, lk2, sub_ref, q_ref, k_ref, v_ref, o_ref,
                        qt_scr, vt_scr, m_scr, acc_scr, sa_scr, sb_scr, *, lam_init, seq, tq):
    lam = _lambda_full(lq1, lk1, lq2, lk2, lam_init)
    heads = range(HEADS_PER_STEP)
    hcols = lambda h: slice(h * V_HEAD_DIM, (h + 1) * V_HEAD_DIM)
    for c in range(seq // tq):
        sl = slice(c * tq, (c + 1) * tq)
        qt_scr[:, sl] = q_ref[sl, :].astype(F32).T.astype(BF16)
        vt = v_ref[sl, :].astype(F32).T.astype(BF16)
        for h in heads:
            vt_scr[h, 0:V_HEAD_DIM, sl] = vt[hcols(h)]
    first_row = lax.broadcasted_iota(jnp.int32, (BF16_ROWS, seq), 0) == 0
    for h in heads:
        vt_scr[h, V_HEAD_DIM:V_HEAD_DIM + BF16_ROWS, :] = \
            jnp.where(first_row, 1.0, 0.0).astype(BF16)

    key = lax.broadcasted_iota(jnp.int32, (tq, 2 * tq), 0)
    col = lax.broadcasted_iota(jnp.int32, (tq, 2 * tq), 1)
    causal = key <= jnp.where(col >= tq, col - tq, col)
    zeros = jnp.zeros((QK_HEAD_DIM, tq), BF16)

    bufs = (sa_scr, sb_scr)

    def produce(qs, buf, start):
        for h in heads:
            buf[h] = jnp.dot(k_ref[pl.ds(start, tq), hcols(h)], qs[h],
                             preferred_element_type=F32)

    def consume(buf, start, masked):
        s = [buf[h] for h in heads]
        if masked:
            s = [jnp.where(causal, sh, NEG_INF) for sh in s]
        m_new = [jnp.maximum(m_scr[h], jnp.max(s[h], axis=0, keepdims=True)) for h in heads]
        for h in heads:
            alpha = jnp.exp2(m_scr[h] - m_new[h])
            p = jnp.exp2(s[h] - m_new[h]).astype(BF16)
            pv = jnp.dot(vt_scr[h, :, pl.ds(start, tq)], p, preferred_element_type=F32)
            acc_scr[h] = alpha * acc_scr[h] + pv
            m_scr[h] = m_new[h]

    for qi in range(seq // tq):
        qs = []
        for h in heads:
            qt = qt_scr[hcols(h), qi * tq:(qi + 1) * tq]
            qs.append(jnp.concatenate(
                [jnp.concatenate([qt[:QK_HEAD_DIM], zeros], axis=0),
                 jnp.concatenate([zeros, qt[QK_HEAD_DIM:]], axis=0)], axis=1))
        m_scr[...] = jnp.full(m_scr.shape, NEG_INF, F32)
        acc_scr[...] = jnp.zeros(acc_scr.shape, F32)

        produce(qs, bufs[0], 0)
        npairs = qi // 2

        def pair(i, carry):
            base = pl.multiple_of(i * (2 * tq), 2 * tq)
            mid = pl.multiple_of(base + tq, tq)
            produce(qs, bufs[1], mid)
            consume(bufs[0], base, False)
            produce(qs, bufs[0], pl.multiple_of(base + 2 * tq, 2 * tq))
            consume(bufs[1], mid, False)
            return carry

        if npairs > 0:
            lax.fori_loop(0, npairs, pair, 0)
        last = 0
        if qi % 2:
            produce(qs, bufs[1], qi * tq)
            consume(bufs[0], (qi - 1) * tq, False)
            last = 1
        consume(bufs[last], qi * tq, True)

        for h in heads:
            acc = acc_scr[h]
            on = acc[0:V_HEAD_DIM] / acc[V_HEAD_DIM:V_HEAD_DIM + 1]
            d = on[:, :tq] - lam * on[:, tq:]
            ms = jnp.mean(d * d, axis=0, keepdims=True)
            y = d * lax.rsqrt(ms + EPS) * sub_ref[...] * (1.0 - lam_init)
            o_ref[qi * tq:(qi + 1) * tq, hcols(h)] = y.T.astype(o_ref.dtype)


def _attn_prompt(lams, sub, q, kb, vb, lam_init, batch, seq, tq=256):
    hp = HEADS_PER_STEP
    assert seq % tq == 0 and N_HEADS % hp == 0
    vec = pl.BlockSpec((1, QK_HEAD_DIM), lambda b, h: (0, 0))
    blk = pl.BlockSpec((seq, hp * V_HEAD_DIM), lambda b, h: (b, h))
    sub_cols = jnp.broadcast_to(sub.reshape(V_HEAD_DIM, 1), (V_HEAD_DIM, tq))
    return pl.pallas_call(
        functools.partial(_attn_prompt_kernel, lam_init=lam_init, seq=seq, tq=tq),
        grid=(batch, N_HEADS // hp),
        in_specs=[vec, vec, vec, vec, pl.BlockSpec((V_HEAD_DIM, tq), lambda b, h: (0, 0)),
                  blk, blk, blk],
        out_specs=blk,
        out_shape=jax.ShapeDtypeStruct((batch * seq, ATTN_WIDTH), BF16),
        scratch_shapes=[pltpu.VMEM((hp * V_HEAD_DIM, seq), BF16),
                        pltpu.VMEM((hp, V_HEAD_DIM + BF16_ROWS, seq), BF16),
                        pltpu.VMEM((hp, 1, 2 * tq), F32),
                        pltpu.VMEM((hp, V_HEAD_DIM + BF16_ROWS, 2 * tq), F32),
                        pltpu.VMEM((hp, tq, 2 * tq), F32),
                        pltpu.VMEM((hp, tq, 2 * tq), F32)],
        compiler_params=_cparams(("parallel", "parallel")),
        name="attn_prompt",
    )(*lams, sub_cols, q, kb, vb)


def _decode_step(step, nsteps, lams, sub_ref, q_ref, kn_ref, vn_ref, k_refs, v_refs, o_ref,
                 m_scr, l_scr, acc_scr, *, lam_init, n_new, n_pages):
    g = PAGES_PER_STEP
    lq1, lk1, lq2, lk2 = lams
    nrow = 2 * N_HEADS * n_new
    q = q_ref[...]
    nt_dims = (((1,), (1,)), ((), ()))

    def head_of_row(shape):
        r = lax.broadcasted_iota(jnp.int32, shape, 0)
        return (r % (N_HEADS * n_new)) // n_new, r % n_new

    def update(s, vmat):
        m = m_scr[...]
        m_new = jnp.maximum(m, jnp.max(s, axis=-1, keepdims=True))
        alpha = jnp.exp2(m - m_new)
        p = jnp.exp2(s - m_new)
        l_scr[...] = alpha * l_scr[...] + jnp.sum(p, axis=-1, keepdims=True)
        acc_scr[...] = alpha * acc_scr[...] + jnp.dot(p.astype(BF16), vmat,
                                                      preferred_element_type=F32)
        m_scr[...] = m_new

    @pl.when(step == 0)
    def _():
        m_scr[...] = jnp.full(m_scr.shape, NEG_INF, F32)
        l_scr[...] = jnp.zeros(l_scr.shape, F32)
        acc_scr[...] = jnp.zeros(acc_scr.shape, F32)
        kn = kn_ref[...].reshape(n_new * N_HEADS, V_HEAD_DIM).astype(BF16)
        vn = vn_ref[...].reshape(n_new * N_HEADS, V_HEAD_DIM).astype(BF16)
        s = lax.dot_general(q, kn, nt_dims, preferred_element_type=F32)
        shape = s.shape
        hrow, trow = head_of_row(shape)
        c = lax.broadcasted_iota(jnp.int32, shape, 1)
        ok = (hrow == c % N_HEADS) & (c // N_HEADS <= trow)
        update(jnp.where(ok, s, NEG_INF), vn)

    ncol = PAGE_SIZE * N_HEADS
    hrow, _ = head_of_row((nrow, ncol))
    ok = hrow == lax.broadcasted_iota(jnp.int32, (nrow, ncol), 1) % N_HEADS
    s_list = []
    m_new = m_scr[...]
    for p in range(g):
        kp = k_refs[p][...].reshape(ncol, V_HEAD_DIM).astype(BF16)
        s = lax.dot_general(q, kp, nt_dims, preferred_element_type=F32)
        s = jnp.where(ok, s, NEG_INF)
        if n_pages % g:
            s = jnp.where(step * g + p < n_pages, s, NEG_INF)
        s_list.append(s)
        m_new = jnp.maximum(m_new, jnp.max(s, axis=-1, keepdims=True))
    alpha = jnp.exp2(m_scr[...] - m_new)
    l = alpha * l_scr[...]
    acc = alpha * acc_scr[...]
    for p in range(g):
        pr = jnp.exp2(s_list[p] - m_new)
        l = l + jnp.sum(pr, axis=-1, keepdims=True)
        vp = v_refs[p][...].reshape(ncol, V_HEAD_DIM).astype(BF16)
        acc = acc + jnp.dot(pr.astype(BF16), vp, preferred_element_type=F32)
    m_scr[...] = m_new
    l_scr[...] = l
    acc_scr[...] = acc

    @pl.when(step == nsteps - 1)
    def _():
        lam = _lambda_full(lq1, lk1, lq2, lk2, lam_init)
        on = acc_scr[...] / l_scr[...]
        half = N_HEADS * n_new
        o = on[:half] - lam * on[half:]
        ms = jnp.mean(o * o, axis=-1, keepdims=True)
        o = o * lax.rsqrt(ms + EPS) * sub_ref[...] * (1.0 - lam_init)
        o_ref[...] = o.astype(o_ref.dtype)


CONV_PAD = 32
CONV_CHUNK = 32
CONV_UNROLL = 4
SUBLANES = 8


def _shift_matrix(nwin):
    span = nwin - SUBLANES
    r = jnp.arange((SUBLANES - 1) * span)
    src = r % span + r // span + 1
    one = (src[:, None] == jnp.arange(nwin)[None, :]).astype(BF16)
    return jnp.concatenate([one, one, one], axis=1)


def _conv_kernel(u_ref, buf_ref, w_ref, b_ref, lg_ref, lb_ref, shift_ref, o_ref, nb_ref, pad_scr,
                 *, t_pad, t_real):
    pad_scr[0:CONV_PAD, :] = buf_ref[...]
    pad_scr[CONV_PAD:CONV_PAD + t_pad, :] = u_ref[...]
    off = CONV_PAD - (CONV_KERNEL - 1)
    rc = min(CONV_CHUNK, t_pad)
    nwin = rc + CONV_PAD
    span = nwin - SUBLANES

    def chunk(c0):
        acc = jnp.broadcast_to(b_ref[...], (rc, CONV_WIDTH))
        win = pad_scr[pl.ds(c0, nwin), :]
        hi = win.astype(BF16)
        r1 = win - hi.astype(F32)
        mid = r1.astype(BF16)
        lo = (r1 - mid.astype(F32)).astype(BF16)
        shifted = jnp.dot(shift_ref[...], jnp.concatenate([hi, mid, lo], axis=0),
                          preferred_element_type=F32)
        for s in range(SUBLANES):
            taps = [j for j in range(CONV_KERNEL) if (off + j) % SUBLANES == s]
            sh = win if s == 0 else shifted[(s - 1) * span:s * span]
            for j in taps:
                a0 = off + j - s
                acc = acc + w_ref[j:j + 1, :] * sh[a0:a0 + rc, :]
        mu = jnp.mean(acc, axis=-1, keepdims=True)
        d = acc - mu
        var = jnp.mean(d * d, axis=-1, keepdims=True)
        y = d * lax.rsqrt(var + EPS) * lg_ref[...] + lb_ref[...]
        o_ref[pl.ds(c0, rc), :] = _silu(y).astype(o_ref.dtype)

    if t_pad == rc:
        chunk(0)
    else:
        def body(ci, carry):
            chunk(pl.multiple_of(ci * rc, rc))
            return carry
        lax.fori_loop(0, t_pad // rc, body, 0, unroll=CONV_UNROLL)
    nb_ref[...] = pad_scr[t_real:t_real + CONV_PAD, :]


def _conv_module(u, buf32, w, b, lg, lb, t_real):
    batch, t_pad, _ = u.shape
    vec = pl.BlockSpec((1, CONV_WIDTH), lambda i: (0, 0))
    shift = _shift_matrix(min(CONV_CHUNK, t_pad) + CONV_PAD)
    return pl.pallas_call(
        functools.partial(_conv_kernel, t_pad=t_pad, t_real=t_real),
        grid=(batch,),
        in_specs=[pl.BlockSpec((None, t_pad, CONV_WIDTH), lambda i: (i, 0, 0)),
                  pl.BlockSpec((None, CONV_PAD, CONV_WIDTH), lambda i: (i, 0, 0)),
                  pl.BlockSpec((CONV_PAD, CONV_WIDTH), lambda i: (0, 0)),
                  vec, vec, vec, pl.BlockSpec(shift.shape, lambda i: (0, 0))],
        out_specs=[pl.BlockSpec((None, t_pad, CONV_WIDTH), lambda i: (i, 0, 0)),
                   pl.BlockSpec((None, CONV_PAD, CONV_WIDTH), lambda i: (i, 0, 0))],
        out_shape=[jax.ShapeDtypeStruct((batch, t_pad, CONV_WIDTH), BF16),
                   jax.ShapeDtypeStruct((batch, CONV_PAD, CONV_WIDTH), F32)],
        scratch_shapes=[pltpu.VMEM((CONV_PAD + t_pad, CONV_WIDTH), F32)],
        compiler_params=_cparams(("parallel",)),
        name="conv_module",
    )(u, buf32, w, b, lg, lb, shift)


LRU_PAD = 8
LRU_CHUNK = 256


def _lru_kernel(gate_ref, xb_ref, buf_ref, h0_ref, cw_ref, cb_ref, wa_ref, ba_ref, wx_ref, bx_ref,
                lam_ref, y_ref, nb_ref, hl_ref, pad_scr, a_scr, b_scr, h_scr, *, t_pad, t_real):
    pad_scr[0:LRU_PAD, :] = buf_ref[...]
    pad_scr[LRU_PAD:LRU_PAD + t_pad, :] = xb_ref[...]
    off = LRU_PAD - (LRU_CONV - 1)
    rc = min(LRU_CHUNK, t_pad)
    lam = lam_ref[...]
    sp = jnp.maximum(-lam, 0.0) + _log1p(jnp.exp(-jnp.abs(lam)))

    def gates(c0):
        xc = jnp.broadcast_to(cb_ref[...], (rc, LRU_WIDTH))
        win = pad_scr[pl.ds(c0, rc + LRU_PAD), :]
        for j in range(LRU_CONV):
            xc = xc + cw_ref[j:j + 1, :] * win[off + j:off + j + rc, :]
        xcb = xc.astype(BF16)
        r = _sigmoid(jnp.dot(xcb, wa_ref[...], preferred_element_type=F32) + ba_ref[...])
        i = _sigmoid(jnp.dot(xcb, wx_ref[...], preferred_element_type=F32) + bx_ref[...])
        log_a = (-LRU_C) * r * sp
        a_scr[pl.ds(c0, rc), :] = jnp.exp(log_a)
        b_scr[pl.ds(c0, rc), :] = jnp.sqrt(-_expm1_nonpos(2.0 * log_a)) * (i * xc)

    if t_pad == rc:
        gates(0)
    else:
        def gbody(ci, carry):
            gates(pl.multiple_of(ci * rc, rc))
            return carry
        lax.fori_loop(0, t_pad // rc, gbody, 0)

    row = lax.broadcasted_iota(jnp.int32, (8, LRU_WIDTH), 0)

    def sbody(gi, h):
        r0 = pl.multiple_of(gi * 8, 8)
        a = a_scr[pl.ds(r0, 8), :]
        b = b_scr[pl.ds(r0, 8), :]
        for s in (1, 2, 4):
            a_sh = jnp.where(row >= s, pltpu.roll(a, s, 0), 1.0)
            b_sh = jnp.where(row >= s, pltpu.roll(b, s, 0), 0.0)
            b = a * b_sh + b
            a = a * a_sh
        hs = a * h + b
        h_scr[pl.ds(r0, 8), :] = hs
        return hs[7:8, :]

    lax.fori_loop(0, t_pad // 8, sbody, h0_ref[...], unroll=min(4, t_pad // 8))
    y_ref[...] = (h_scr[...] * gate_ref[...]).astype(y_ref.dtype)
    nb_ref[...] = pad_scr[t_real:t_real + LRU_PAD, :]
    hl_ref[...] = h_scr[t_real - 1:t_real, :]


def _lru_block(gate, xb, buf8, h0, cw, cb, wa, ba, wx, bx, lam, layer, t_real):
    batch, t_pad, _ = gate.shape
    vec = pl.BlockSpec((1, LRU_WIDTH), lambda i: (0, 0))
    seq = pl.BlockSpec((None, t_pad, LRU_WIDTH), lambda i: (i, 0, 0))
    mat = pl.BlockSpec((None, LRU_WIDTH, LRU_WIDTH), lambda i: (layer, 0, 0))
    return pl.pallas_call(
        functools.partial(_lru_kernel, t_pad=t_pad, t_real=t_real),
        grid=(batch,),
        in_specs=[seq, seq,
                  pl.BlockSpec((None, LRU_PAD, LRU_WIDTH), lambda i: (i, 0, 0)),
                  pl.BlockSpec((None, 1, LRU_WIDTH), lambda i: (i, 0, 0)),
                  pl.BlockSpec((LRU_CONV, LRU_WIDTH), lambda i: (0, 0)),
                  vec, mat, vec, mat, vec, vec],
        out_specs=[seq,
                   pl.BlockSpec((None, LRU_PAD, LRU_WIDTH), lambda i: (i, 0, 0)),
                   pl.BlockSpec((None, 1, LRU_WIDTH), lambda i: (i, 0, 0))],
        out_shape=[jax.ShapeDtypeStruct((batch, t_pad, LRU_WIDTH), BF16),
                   jax.ShapeDtypeStruct((batch, LRU_PAD, LRU_WIDTH), F32),
                   jax.ShapeDtypeStruct((batch, 1, LRU_WIDTH), F32)],
        scratch_shapes=[pltpu.VMEM((LRU_PAD + t_pad, LRU_WIDTH), F32),
                        pltpu.VMEM((t_pad, LRU_WIDTH), F32),
                        pltpu.VMEM((t_pad, LRU_WIDTH), F32),
                        pltpu.VMEM((t_pad, LRU_WIDTH), F32)],
        compiler_params=_cparams(("parallel",)),
        name="lru_block",
    )(gate, xb, buf8, h0, cw, cb, wa, ba, wx, bx, lam)


def _out_proj_kernel(x_ref, oa_ref, oc_ref, ol_ref, w_ref, y_ref):
    acc = x_ref[...]
    acc = acc + jnp.dot(oa_ref[...], w_ref[0:ATTN_WIDTH, :], preferred_element_type=F32)
    acc = acc + jnp.dot(oc_ref[...], w_ref[ATTN_WIDTH:ATTN_WIDTH + CONV_WIDTH, :],
                        preferred_element_type=F32)
    acc = acc + jnp.dot(ol_ref[...], w_ref[ATTN_WIDTH + CONV_WIDTH:, :],
                        preferred_element_type=F32)
    y_ref[...] = acc


def _out_proj(x, oa, oc, ol, w, layer):
    m = x.shape[0]
    tm, _ = _tiles(m)
    assert m % tm == 0
    row = lambda i: (i, 0)
    return pl.pallas_call(
        _out_proj_kernel,
        grid=(m // tm,),
        in_specs=[pl.BlockSpec((tm, D_MODEL), row), pl.BlockSpec((tm, ATTN_WIDTH), row),
                  pl.BlockSpec((tm, CONV_WIDTH), row), pl.BlockSpec((tm, LRU_WIDTH), row),
                  pl.BlockSpec((None, D_MODEL, D_MODEL), lambda i: (layer, 0, 0))],
        out_specs=pl.BlockSpec((tm, D_MODEL), row),
        out_shape=jax.ShapeDtypeStruct((m, D_MODEL), F32),
        compiler_params=_cparams(("parallel",)),
        name="out_proj",
    )(x, oa, oc, ol, w)


def _ffn_step(f, x_ref, g_ref, wg_ref, wu_ref, wd_ref, y_ref, xn_scr):
    @pl.when(f == 0)
    def _():
        x = x_ref[...]
        ms = jnp.mean(x * x, axis=-1, keepdims=True)
        xn_scr[...] = (x * lax.rsqrt(ms + EPS) * g_ref[...]).astype(BF16)
        y_ref[...] = x

    xn = xn_scr[...]
    gate = jnp.dot(xn, wg_ref[...], preferred_element_type=F32)
    up = jnp.dot(xn, wu_ref[...], preferred_element_type=F32)
    hid = (_silu(gate) * up).astype(BF16)
    y_ref[...] += jnp.dot(hid, wd_ref[...], preferred_element_type=F32)


def _ffn_kernel(x_ref, g_ref, wg_ref, wu_ref, wd_ref, y_ref, xn_scr):
    _ffn_step(pl.program_id(1), x_ref, g_ref, wg_ref, wu_ref, wd_ref, y_ref, xn_scr)


def _ffn_decode_kernel(pt_ref, x_ref, g_ref, wg_ref, wu_ref, wd_ref,
                       lq1, lk1, lq2, lk2, sub_ref, q_ref, kn_ref, vn_ref, *rest,
                       lam_init, n_new, nsteps, n_pages):
    del pt_ref
    g = PAGES_PER_STEP
    k_refs, v_refs = rest[:g], rest[g:2 * g]
    y_ref, o_ref, xn_scr, m_scr, l_scr, acc_scr = rest[2 * g:]
    f = pl.program_id(1)
    _ffn_step(f, x_ref, g_ref, wg_ref, wu_ref, wd_ref, y_ref, xn_scr)

    @pl.when(f < nsteps)
    def _():
        _decode_step(f, nsteps, (lq1, lk1, lq2, lk2), sub_ref, q_ref, kn_ref, vn_ref, k_refs, v_refs,
                     o_ref, m_scr, l_scr, acc_scr, lam_init=lam_init, n_new=n_new,
                     n_pages=n_pages)


def _ffn_decode(x, g, wg, wu, wd, layer, page_table, lams, sub, q_rows, k_new, v_new,
                cache_k, cache_v, lam_init):
    m = x.shape[0]
    tm, tf = _tiles(m)
    d_ff = wd.shape[1]
    assert wg.shape[1:] == (d_ff // tf, D_MODEL, tf)
    batch, n_pages = page_table.shape
    n_new = k_new.shape[1]
    pg = PAGES_PER_STEP
    nrow = 2 * N_HEADS * n_new
    nsteps = -(-n_pages // pg)
    assert m % tm == 0 and d_ff % tf == 0
    assert m // tm == batch and d_ff // tf >= nsteps
    row = lambda i, f, pt: (i, 0)
    const = lambda i, f, pt: (0, 0)
    seq3 = lambda i, f, pt: (i, 0, 0)
    new_spec = pl.BlockSpec((None, n_new, N_HEADS, V_HEAD_DIM), lambda i, f, pt: (i, 0, 0, 0))

    def page_spec(r):
        def index(i, f, pt):
            slot = jnp.minimum(jnp.minimum(f, nsteps - 1) * pg + r, n_pages - 1)
            return (layer, pt[i, slot], 0, 0, 0)
        return pl.BlockSpec((None, None, PAGE_SIZE, N_HEADS, V_HEAD_DIM), index)

    vec = pl.BlockSpec((1, QK_HEAD_DIM), const)
    in_specs = ([pl.BlockSpec((tm, D_MODEL), row),
                 pl.BlockSpec((1, D_MODEL), const),
                 pl.BlockSpec((None, None, D_MODEL, tf), lambda i, f, pt: (layer, f, 0, 0)),
                 pl.BlockSpec((None, None, D_MODEL, tf), lambda i, f, pt: (layer, f, 0, 0)),
                 pl.BlockSpec((None, tf, D_MODEL), lambda i, f, pt: (layer, f, 0)),
                 vec, vec, vec, vec, pl.BlockSpec((1, V_HEAD_DIM), const),
                 pl.BlockSpec((None, nrow, V_HEAD_DIM), seq3), new_spec, new_spec]
                + [page_spec(r) for r in range(pg)] + [page_spec(r) for r in range(pg)])
    grid_spec = pltpu.PrefetchScalarGridSpec(
        num_scalar_prefetch=1,
        grid=(m // tm, d_ff // tf),
        in_specs=in_specs,
        out_specs=[pl.BlockSpec((tm, D_MODEL), row),
                   pl.BlockSpec((None, N_HEADS * n_new, V_HEAD_DIM), seq3)],
        scratch_shapes=[pltpu.VMEM((tm, D_MODEL), BF16),
                        pltpu.VMEM((nrow, 1), F32), pltpu.VMEM((nrow, 1), F32),
                        pltpu.VMEM((nrow, V_HEAD_DIM), F32)],
    )
    return pl.pallas_call(
        functools.partial(_ffn_decode_kernel, lam_init=lam_init, n_new=n_new, nsteps=nsteps,
                          n_pages=n_pages),
        grid_spec=grid_spec,
        out_shape=[jax.ShapeDtypeStruct((m, D_MODEL), F32),
                   jax.ShapeDtypeStruct((batch, N_HEADS * n_new, V_HEAD_DIM), BF16)],
        compiler_params=_cparams(("parallel", "arbitrary")),
        name="ffn_decode",
    )(page_table, x, g, wg, wu, wd, *lams, sub, q_rows, k_new, v_new,
      *([cache_k] * pg), *([cache_v] * pg))


def _ffn(x, g, wg, wu, wd, layer):
    m = x.shape[0]
    tm, tf = _tiles(m)
    d_ff = wd.shape[1]
    assert wg.shape[1:] == (d_ff // tf, D_MODEL, tf)
    assert m % tm == 0 and d_ff % tf == 0
    row = lambda i, f: (i, 0)
    return pl.pallas_call(
        _ffn_kernel,
        grid=(m // tm, d_ff // tf),
        in_specs=[pl.BlockSpec((tm, D_MODEL), row),
                  pl.BlockSpec((1, D_MODEL), lambda i, f: (0, 0)),
                  pl.BlockSpec((None, None, D_MODEL, tf), lambda i, f: (layer, f, 0, 0)),
                  pl.BlockSpec((None, None, D_MODEL, tf), lambda i, f: (layer, f, 0, 0)),
                  pl.BlockSpec((None, tf, D_MODEL), lambda i, f: (layer, f, 0))],
        out_specs=pl.BlockSpec((tm, D_MODEL), row),
        out_shape=jax.ShapeDtypeStruct((m, D_MODEL), F32),
        scratch_shapes=[pltpu.VMEM((tm, D_MODEL), BF16)],
        compiler_params=_cparams(("parallel", "arbitrary")),
        name="ffn",
    )(x, g, wg, wu, wd)


def _rope_tables(pos):
    half = ROPE_DIM // 2
    inv_freq = ROPE_THETA ** (-jnp.arange(0, ROPE_DIM, 2, dtype=F32) / ROPE_DIM)
    ang = pos.astype(F32)[:, None] * inv_freq[None, :]
    cos, sin = jnp.cos(ang), jnp.sin(ang)
    t = pos.shape[0]
    ones = jnp.ones((t, QK_HEAD_DIM - ROPE_DIM), F32)
    zeros = jnp.zeros((t, QK_HEAD_DIM - ROPE_DIM), F32)
    zh = jnp.zeros((t, half), F32)
    rc = jnp.concatenate([cos, cos, ones], axis=1)
    rs1 = jnp.concatenate([-sin, zh, zeros], axis=1)
    rs2 = jnp.concatenate([zh, sin, zeros], axis=1)
    return tuple(jnp.tile(a, (1, MXU_COLS // QK_HEAD_DIM)) for a in (rc, rs1, rs2))


def _block_diag(w):
    h, di, dj = w.shape
    eye = jnp.eye(h, dtype=w.dtype)
    return (eye[:, None, :, None] * w[:, :, None, :]).reshape(h * di, h * dj)


def _hidden_tile_major(w):
    layers, d_in, d_ff = w.shape
    tiles = w.reshape(layers, d_in, d_ff // FFN_TILE, FFN_TILE).transpose(0, 2, 1, 3)
    return tiles.astype(BF16)


def _group_mean_matrix():
    r = jnp.arange(MXU_COLS) // QK_HEAD_DIM
    return ((r[:, None] == r[None, :]).astype(F32) / QK_HEAD_DIM).astype(BF16)


def _row(a, l):
    return a[l].reshape(1, -1)


def _lam_init(l):
    return 0.8 - 0.6 * math.exp(-0.3 * l)


def _lambda_vectors(p, l):
    return tuple(_row(p[n], l) for n in ('lambda_q1', 'lambda_k1', 'lambda_q2', 'lambda_k2'))


def _rope_for(start_pos, t, rows):
    tm = _in_proj_tile(rows)
    rope = _rope_tables(start_pos + jnp.arange(t, dtype=jnp.int32))
    if t < tm:
        rope = tuple(jnp.tile(a, (tm // t, 1)) for a in rope)
    return rope


def _project(x, p, l, rope):
    gain = lambda a: jnp.tile(a[l], MXU_COLS // QK_HEAD_DIM).reshape(1, MXU_COLS)
    return _in_proj(x, _row(p['norm_mix'], l), p['w_in_bf'], l, gain(p['q_norm']),
                    gain(p['k_norm']), p['gmat'], *rope)


def _branches(u, gate, xb, conv_buf, lru_buf, h0, p, l, batch, t_real):
    m = u.shape[0]

    def seq3(a, rows):
        a = a.reshape(batch, t_real, a.shape[-1])
        pad = -t_real % rows
        return a if pad == 0 else jnp.pad(a, ((0, 0), (0, pad), (0, 0)))

    conv_w = jnp.pad(p['conv_w'][l], ((0, CONV_PAD - CONV_KERNEL), (0, 0)))
    o_conv, nb_conv = _conv_module(seq3(u, BF16_ROWS), conv_buf, conv_w, _row(p['conv_b'], l),
                                   _row(p['conv_ln_g'], l), _row(p['conv_ln_b'], l), t_real)
    o_lru, nb_lru, h_last = _lru_block(
        seq3(gate, SUBLANES), seq3(xb, SUBLANES), lru_buf, h0, p['lru_conv_w'][l],
        _row(p['lru_conv_b'], l), p['lru_wa_bd'], _row(p['lru_ba'], l), p['lru_wx_bd'],
        _row(p['lru_bx'], l), _row(p['lru_lambda'], l), l, t_real)
    states = (nb_conv[:, CONV_PAD - (CONV_KERNEL - 1):],
              nb_lru[:, LRU_PAD - (LRU_CONV - 1):],
              h_last.reshape(batch, LRU_WIDTH))
    return (o_conv[:, :t_real].reshape(m, CONV_WIDTH), o_lru[:, :t_real].reshape(m, LRU_WIDTH),
            states)


def _decode_queries(q, batch, t):
    q5 = q.reshape(batch, t, N_HEADS, 2, QK_HEAD_DIM)
    zero = jnp.zeros_like(q5[:, :, :, 0])
    q_rows = jnp.stack([jnp.concatenate([q5[:, :, :, 0], zero], axis=-1),
                        jnp.concatenate([zero, q5[:, :, :, 1]], axis=-1)], axis=1)
    return q_rows.transpose(0, 1, 3, 2, 4).reshape(batch, 2 * N_HEADS * t, V_HEAD_DIM)


def kernel(x_prompt, x_sample, cache_k, cache_v, state_conv, state_lru_conv, state_lru_h, page_table,
           norm_mix, w_in, q_norm, k_norm, lambda_q1, lambda_k1, lambda_q2, lambda_k2, subln,
           conv_w, conv_b, conv_ln_g, conv_ln_b, lru_conv_w, lru_conv_b, lru_wa, lru_ba, lru_wx, lru_bx,
           lru_lambda, w_out, norm_ffn, w_ffn_gate, w_ffn_up, w_ffn_down):
    p = dict(norm_mix=norm_mix, q_norm=q_norm, k_norm=k_norm,
             lambda_q1=lambda_q1, lambda_k1=lambda_k1, lambda_q2=lambda_q2, lambda_k2=lambda_k2,
             subln=subln, conv_w=conv_w, conv_b=conv_b, conv_ln_g=conv_ln_g, conv_ln_b=conv_ln_b,
             lru_conv_w=lru_conv_w, lru_conv_b=lru_conv_b, lru_ba=lru_ba, lru_bx=lru_bx,
             lru_lambda=lru_lambda, norm_ffn=norm_ffn)
    p['w_in_bf'] = w_in.astype(BF16)
    p['w_out_bf'] = w_out.astype(BF16)
    p['w_gate_bf'] = _hidden_tile_major(w_ffn_gate)
    p['w_up_bf'] = _hidden_tile_major(w_ffn_up)
    p['w_down_bf'] = w_ffn_down.astype(BF16)
    p['lru_wa_bd'] = jax.vmap(_block_diag)(lru_wa).astype(BF16)
    p['lru_wx_bd'] = jax.vmap(_block_diag)(lru_wx).astype(BF16)
    p['gmat'] = _group_mean_matrix()

    depth = w_in.shape[0]
    ffn_w = (p['w_gate_bf'], p['w_up_bf'], p['w_down_bf'])

    bp, tp, _ = x_prompt.shape
    bs, ts, _ = x_sample.shape
    mp, ms = bp * tp, bs * ts
    past_len = page_table.shape[1] * PAGE_SIZE
    rope_p = _rope_for(0, tp, mp)
    rope_s = _rope_for(past_len, ts, ms)
    xp = x_prompt.reshape(mp, D_MODEL)
    xs = x_sample.reshape(ms, D_MODEL)
    outs_p, outs_s = [], []
    for l in range(depth):
        lam_init = _lam_init(l)
        lams = _lambda_vectors(p, l)
        sub = _row(p['subln'], l)

        q, k, kb, v, vb, u, gate, xb = _project(xp, p, l, rope_p)
        o_attn = _attn_prompt(lams, sub, q, kb, vb, lam_init, bp, tp)
        o_conv, o_lru, st = _branches(
            u, gate, xb, jnp.zeros((bp, CONV_PAD, CONV_WIDTH), F32),
            jnp.zeros((bp, LRU_PAD, LRU_WIDTH), F32), jnp.zeros((bp, 1, LRU_WIDTH), F32),
            p, l, bp, tp)
        xp = _out_proj(xp, o_attn, o_conv, o_lru, p['w_out_bf'], l)
        outs_p.append((k.reshape(bp, tp, N_HEADS, V_HEAD_DIM),
                       v.reshape(bp, tp, N_HEADS, V_HEAD_DIM)) + st)

        q, k, kb, v, vb, u, gate, xb = _project(xs, p, l, rope_s)
        k_new = k.reshape(bs, ts, N_HEADS, V_HEAD_DIM)
        v_new = v.reshape(bs, ts, N_HEADS, V_HEAD_DIM)
        xp, o = _ffn_decode(xp, _row(p['norm_ffn'], l), *ffn_w, l, page_table, lams, sub,
                            _decode_queries(q, bs, ts), k_new, v_new, cache_k, cache_v, lam_init)
        o_attn = o.reshape(bs, N_HEADS, ts, V_HEAD_DIM).transpose(0, 2, 1, 3).reshape(ms, ATTN_WIDTH)
        o_conv, o_lru, st = _branches(
            u, gate, xb,
            jnp.pad(state_conv[l], ((0, 0), (CONV_PAD - (CONV_KERNEL - 1), 0), (0, 0))),
            jnp.pad(state_lru_conv[l], ((0, 0), (LRU_PAD - (LRU_CONV - 1), 0), (0, 0))),
            state_lru_h[l].reshape(bs, 1, LRU_WIDTH), p, l, bs, ts)
        xs = _out_proj(xs, o_attn, o_conv, o_lru, p['w_out_bf'], l)
        xs = _ffn(xs, _row(p['norm_ffn'], l), *ffn_w, l)
        outs_s.append((k_new, v_new) + st)

    stack = lambda outs: tuple(jnp.stack([o[i] for o in outs]) for i in range(5))
    k_p, v_p, cb_p, lcb_p, h_p = stack(outs_p)
    k_s, v_s, cb_s, lcb_s, h_s = stack(outs_s)
    return (xp.reshape(bp, tp, D_MODEL), xs.reshape(bs, ts, D_MODEL),
            k_p, v_p, cb_p, lcb_p, h_p, k_s, v_s, cb_s, lcb_s, h_s)
```
